```python
import jax, jax.numpy as jnp
from jax import lax
import numpy as np

D_MODEL = 1024
BATCH = 4
SEQ = 4096
DEPTH = 2
DEC_BATCH = 2
DEC_SEQ = 8192
PAST_LEN = 128

D_MIX = 2 * D_MODEL
D_SSD = D_MIX // 2
D_CM = D_MIX - D_SSD
SSD_HEAD_DIM = 64
N_SSD_HEADS = D_SSD // SSD_HEAD_DIM
N_SSD_GROUPS = 2
D_STATE = 128
SSD_CONV = 5
CHUNK = 128
D_XBC = D_SSD + 2 * N_SSD_GROUPS * D_STATE
CM_CONV = 31
D_FF = 4 * D_MODEL
D_IN_PROJ = D_SSD + D_XBC + 2 * N_SSD_HEADS + 2 * D_CM
EPS = 1e-5

kernel_name = "hymba_ssd_conformer_encoder"


def rmsnorm(x, w):
    xf = x.astype(jnp.float32)
    y = xf * lax.rsqrt(jnp.mean(xf * xf, axis=-1, keepdims=True) + EPS)
    return (y * w.astype(jnp.float32)).astype(x.dtype)


def layernorm(x, w, b):
    xf = x.astype(jnp.float32)
    mu = jnp.mean(xf, axis=-1, keepdims=True)
    xc = xf - mu
    y = xc * lax.rsqrt(jnp.mean(xc * xc, axis=-1, keepdims=True) + EPS)
    return (y * w.astype(jnp.float32) + b.astype(jnp.float32)).astype(x.dtype)


def dwconv(x, w, b):
    K, C = w.shape
    y = lax.conv_general_dilated(
        x, w[:, None, :].astype(x.dtype), window_strides=(1,),
        padding=[(K // 2, K // 2)], dimension_numbers=("NWC", "WIO", "NWC"),
        feature_group_count=C)
    return y + b.astype(x.dtype)


def ssd_chunked(x, dt, A, Bm, Cm):
    b, L, h, p = x.shape
    g, n = Bm.shape[2], Bm.shape[3]
    j = h // g
    nc = L // CHUNK
    xd = (x * dt[..., None]).reshape(b, nc, CHUNK, g, j, p)
    dA = (dt * A).reshape(b, nc, CHUNK, g, j).transpose(0, 3, 4, 1, 2)
    Bc = Bm.reshape(b, nc, CHUNK, g, n)
    Cc = Cm.reshape(b, nc, CHUNK, g, n)
    acs = jnp.cumsum(dA, axis=-1)
    seg = acs[..., :, None] - acs[..., None, :]
    tri = jnp.tril(jnp.ones((CHUNK, CHUNK), dtype=bool))
    lmat = jnp.exp(jnp.where(tri, seg, -jnp.inf))
    cb = jnp.einsum("bclgn,bcsgn->bgcls", Cc, Bc)
    y_diag = jnp.einsum("bgcls,bgjcls,bcsgjp->bclgjp", cb, lmat, xd)
    decay_states = jnp.exp(acs[..., -1:] - acs)
    states = jnp.einsum("bclgn,bgjcl,bclgjp->bcgjpn", Bc, decay_states, xd)
    chunk_decay = jnp.exp(acs[..., -1])

    def step(hprev, inp):
        dec, st = inp
        return hprev * dec[..., None, None] + st, hprev

    h0 = jnp.zeros((b, g, j, p, n), jnp.float32)
    _, prev_states = lax.scan(step, h0, (jnp.moveaxis(chunk_decay, -1, 0),
                                         jnp.moveaxis(states, 1, 0)))
    y_off = jnp.einsum("bclgn,cbgjpn,bgjcl->bclgjp", Cc, prev_states, jnp.exp(acs))
    return (y_diag + y_off).reshape(b, L, h, p)


def ssd_mixer(z, xbc, dt_raw, conv_w, conv_b, dt_bias, a_log, d_skip, norm_w):
    b, L, _ = z.shape
    xbc = jax.nn.silu(dwconv(xbc, conv_w, conv_b)).astype(jnp.float32)
    xs = xbc[..., :D_SSD].reshape(b, L, N_SSD_HEADS, SSD_HEAD_DIM)
    gn = N_SSD_GROUPS * D_STATE
    Bm = xbc[..., D_SSD:D_SSD + gn].reshape(b, L, N_SSD_GROUPS, D_STATE)
    Cm = xbc[..., D_SSD + gn:].reshape(b, L, N_SSD_GROUPS, D_STATE)
    dt = jax.nn.softplus(dt_raw.astype(jnp.float32).reshape(b, L, 2, N_SSD_HEADS)
                         + dt_bias.astype(jnp.float32))
    A = -jnp.exp(a_log.astype(jnp.float32))
    y_f = ssd_chunked(xs, dt[:, :, 0], A[0], Bm, Cm)
    y_b = jnp.flip(ssd_chunked(jnp.flip(xs, 1), jnp.flip(dt[:, :, 1], 1), A[1],
                               jnp.flip(Bm, 1), jnp.flip(Cm, 1)), 1)
    y = y_f + y_b + xs * d_skip.astype(jnp.float32)[:, None]
    y = y.reshape(b, L, D_SSD) * jax.nn.silu(z.astype(jnp.float32))
    yg = y.reshape(b, L, N_SSD_GROUPS, D_SSD // N_SSD_GROUPS)
    yg = yg * lax.rsqrt(jnp.mean(yg * yg, axis=-1, keepdims=True) + EPS)
    y = yg.reshape(b, L, D_SSD) * norm_w.astype(jnp.float32)
    return y.astype(z.dtype)


def conformer_conv(u, dw_w, dw_b, ln_w, ln_b):
    v, gate = jnp.split(u, 2, axis=-1)
    h = v * jax.nn.sigmoid(gate)
    h = dwconv(h, dw_w, dw_b)
    h = layernorm(h, ln_w, ln_b)
    return jax.nn.silu(h)


def trunk(x, norm1_w, w_in, ssd_conv_w, ssd_conv_b, dt_bias, a_log, d_skip, ssd_norm_w,
          cm_dw_w, cm_dw_b, cm_ln_w, cm_ln_b, w_out, norm2_w, w_mlp_in, w_mlp_out,
          final_norm_w):
    s1 = D_SSD
    s2 = s1 + D_XBC
    s3 = s2 + 2 * N_SSD_HEADS
    for l in range(DEPTH):
        n = rmsnorm(x, norm1_w[l])
        proj = n @ w_in[l].astype(n.dtype)
        z, xbc, dt_raw, cm_in = proj[..., :s1], proj[..., s1:s2], proj[..., s2:s3], proj[..., s3:]
        y_ssd = ssd_mixer(z, xbc, dt_raw, ssd_conv_w[l], ssd_conv_b[l], dt_bias[l],
                          a_log[l], d_skip[l], ssd_norm_w[l])
        y_cm = conformer_conv(cm_in, cm_dw_w[l], cm_dw_b[l], cm_ln_w[l], cm_ln_b[l])
        x = x + jnp.concatenate([y_ssd, y_cm], axis=-1) @ w_out[l].astype(x.dtype)
        n2 = rmsnorm(x, norm2_w[l])
        x = x + jnp.square(jax.nn.relu(n2 @ w_mlp_in[l].astype(x.dtype))) @ w_mlp_out[l].astype(x.dtype)
    return rmsnorm(x, final_norm_w)


def setup_inputs(seed: int = 0) -> dict:
    key = jax.random.key(seed)
    ks = jax.random.split(key, 20)
    f32 = jnp.float32
    nrm = lambda k, shape, s: jax.random.normal(k, shape, f32) * s
    dt0 = jnp.exp(jax.random.uniform(ks[5], (DEPTH, 2, N_SSD_HEADS), f32,
                                     np.log(1e-3), np.log(1e-1)))
    return {
        "x_prompt": nrm(ks[0], (BATCH, SEQ, D_MODEL), 1.0),
        "x_sample": nrm(ks[1], (DEC_BATCH, DEC_SEQ, D_MODEL), 1.0),
        "norm1_w": 1.0 + nrm(ks[2], (DEPTH, D_MODEL), 0.02),
        "w_in": nrm(ks[3], (DEPTH, D_MODEL, D_IN_PROJ), D_MODEL ** -0.5),
        "ssd_conv_w": nrm(ks[4], (DEPTH, SSD_CONV, D_XBC), SSD_CONV ** -0.5),
        "ssd_conv_b": nrm(ks[6], (DEPTH, D_XBC), 0.02),
        "dt_bias": dt0 + jnp.log(-jnp.expm1(-dt0)),
        "a_log": jnp.log(jax.random.uniform(ks[7], (DEPTH, 2, N_SSD_HEADS), f32, 1.0, 16.0)),
        "d_skip": 1.0 + nrm(ks[8], (DEPTH, N_SSD_HEADS), 0.02),
        "ssd_norm_w": 1.0 + nrm(ks[9], (DEPTH, D_SSD), 0.02),
        "cm_dw_w": nrm(ks[10], (DEPTH, CM_CONV, D_CM), CM_CONV ** -0.5),
        "cm_dw_b": nrm(ks[11], (DEPTH, D_CM), 0.02),
        "cm_ln_w": 1.0 + nrm(ks[12], (DEPTH, D_CM), 0.02),
        "cm_ln_b": nrm(ks[13], (DEPTH, D_CM), 0.02),
        "w_out": nrm(ks[14], (DEPTH, D_MIX, D_MODEL), D_MIX ** -0.5),
        "norm2_w": 1.0 + nrm(ks[15], (DEPTH, D_MODEL), 0.02),
        "w_mlp_in": nrm(ks[16], (DEPTH, D_MODEL, D_FF), D_MODEL ** -0.5),
        "w_mlp_out": nrm(ks[17], (DEPTH, D_FF, D_MODEL), D_FF ** -0.5),
        "final_norm_w": 1.0 + nrm(ks[18], (D_MODEL,), 0.02),
    }


def reference(x_prompt, x_sample, norm1_w, w_in, ssd_conv_w, ssd_conv_b, dt_bias, a_log,
              d_skip, ssd_norm_w, cm_dw_w, cm_dw_b, cm_ln_w, cm_ln_b, w_out, norm2_w,
              w_mlp_in, w_mlp_out, final_norm_w):
    y_prompt = trunk(x_prompt, norm1_w, w_in, ssd_conv_w, ssd_conv_b, dt_bias, a_log, d_skip,
                     ssd_norm_w, cm_dw_w, cm_dw_b, cm_ln_w, cm_ln_b, w_out, norm2_w,
                     w_mlp_in, w_mlp_out, final_norm_w)
    y_sample = trunk(x_sample, norm1_w, w_in, ssd_conv_w, ssd_conv_b, dt_bias, a_log, d_skip,
                     ssd_norm_w, cm_dw_w, cm_dw_b, cm_ln_w, cm_ln_b, w_out, norm2_w,
                     w_mlp_in, w_mlp_out, final_norm_w)
    return (y_prompt, y_sample)
```

```python
import functools

import jax
import jax.numpy as jnp
import numpy as np
from jax import lax
from jax.experimental import pallas as pl
from jax.experimental.pallas import tpu as pltpu

D_MODEL = 1024
D_SSD = 1024
HEAD_DIM = 64
N_HEADS = 16
N_GROUPS = 2
D_STATE = 128
CHUNK = 128
D_XBC = D_SSD + 2 * N_GROUPS * D_STATE
D_CM = 1024
SSD_CONV = 5
CM_CONV = 31
D_FF = 4096
EPS = 1e-5

LANES = 128
HALO = 16
DT_PAD = LANES
VMEM_LIMIT = 56 * 1024 * 1024

F32 = jnp.float32
BF16 = jnp.bfloat16


def _dot(a, b):
    return jnp.dot(a, b, preferred_element_type=F32)


def _sigmoid(x):
    return 1.0 / (1.0 + jnp.exp(-x))


def _any_eq(j, values):
    return functools.reduce(jnp.logical_or, [j == v for v in values])


def _inproj_kernel(x_ref, nw_ref, wz_ref, wx_ref, wcm_ref, dtb_ref,
                   z_ref, xbc_ref, dt_ref, h_ref):
    x = x_ref[...]
    ms = jnp.mean(x * x, axis=-1, keepdims=True)
    n = (x * lax.rsqrt(ms + EPS) * nw_ref[...]).astype(BF16)
    z_ref[...] = _dot(n, wz_ref[...]).astype(BF16)
    xd = _dot(n, wx_ref[...])
    xbc_ref[...] = xd[:, :D_XBC].astype(BF16)
    v = xd[:, D_XBC:] + dtb_ref[...]
    dt_ref[...] = jnp.maximum(v, 0.0) + jnp.log(1.0 + jnp.exp(-jnp.abs(v)))
    u = _dot(n, wcm_ref[...])
    h_ref[...] = (u[:, :D_CM] * _sigmoid(u[:, D_CM:])).astype(BF16)


def _inproj(x, nw, wz, wx, wcm, dtb, tm):
    t = x.shape[0]
    const = lambda i: (0, 0)
    row = lambda i: (i, 0)
    return pl.pallas_call(
        _inproj_kernel,
        grid=(t // tm,),
        in_specs=[
            pl.BlockSpec((tm, D_MODEL), row),
            pl.BlockSpec((1, D_MODEL), const),
            pl.BlockSpec((D_MODEL, D_SSD), const),
            pl.BlockSpec((D_MODEL, D_XBC + DT_PAD), const),
            pl.BlockSpec((D_MODEL, 2 * D_CM), const),
            pl.BlockSpec((1, DT_PAD), const),
        ],
        out_specs=[
            pl.BlockSpec((tm, D_SSD), row),
            pl.BlockSpec((tm, D_XBC), row),
            pl.BlockSpec((tm, DT_PAD), row),
            pl.BlockSpec((tm, D_CM), row),
        ],
        out_shape=[
            jax.ShapeDtypeStruct((t, D_SSD), BF16),
            jax.ShapeDtypeStruct((t, D_XBC), BF16),
            jax.ShapeDtypeStruct((t, DT_PAD), F32),
            jax.ShapeDtypeStruct((t, D_CM), BF16),
        ],
        compiler_params=pltpu.CompilerParams(
            dimension_semantics=("arbitrary",), vmem_limit_bytes=VMEM_LIMIT),
        name="inproj",
    )(x, nw, wz, wx, wcm, dtb)


def _ssd_chunk(act, dt, a_row, h_ref, drow, reverse):
    xs = act[:, :D_SSD]
    xs_bf = xs.astype(BF16)
    a = dt * a_row
    li = lax.broadcasted_iota(jnp.int32, (CHUNK, CHUNK), 0)
    si = lax.broadcasted_iota(jnp.int32, (CHUNK, CHUNK), 1)
    mask = (si >= li) if reverse else (si <= li)
    tmat = jnp.where(mask, 1.0, 0.0).astype(BF16)
    a1 = a.astype(BF16)
    rem = a - a1.astype(F32)
    a2 = rem.astype(BF16)
    a3 = (rem - a2.astype(F32)).astype(BF16)
    parts = _dot(tmat, jnp.concatenate([a1, a2, a3], axis=1))
    acs = parts[:, :DT_PAD] + parts[:, DT_PAD:2 * DT_PAD] + parts[:, 2 * DT_PAD:]
    tot = acs[0:1, :] if reverse else acs[CHUNK - 1:CHUNK, :]
    w2 = dt * jnp.exp(tot - acs)
    acs_t = acs.T
    dt_t = dt.T
    w2_t = w2.T
    lane_lo = lax.broadcasted_iota(jnp.int32, (CHUNK, LANES), 1) < HEAD_DIM
    hoff = N_HEADS if reverse else 0
    heads_per_group = N_HEADS // N_GROUPS
    gw = heads_per_group * HEAD_DIM

    pieces = []
    for g in range(N_GROUPS):
        bg = act[:, D_SSD + g * D_STATE:D_SSD + (g + 1) * D_STATE]
        cg = act[:, D_SSD + N_GROUPS * D_STATE + g * D_STATE:
                 D_SSD + N_GROUPS * D_STATE + (g + 1) * D_STATE]
        bg_bf = bg.astype(BF16)
        cg_bf = cg.astype(BF16)
        cb = lax.dot_general(cg_bf, bg_bf, (((1,), (1,)), ((), ())),
                             preferred_element_type=F32)
        bg_t = bg.T
        yo = _dot(cg_bf, h_ref[:, g * gw:(g + 1) * gw].astype(BF16))
        for pp in range(heads_per_group // 2):
            lanes = slice(g * gw + pp * LANES, g * gw + (pp + 1) * LANES)
            mats = []
            for hh in range(2):
                hl = hoff + g * heads_per_group + 2 * pp + hh
                colb = jnp.broadcast_to(acs[:, hl:hl + 1], (CHUNK, CHUNK))
                rowb = jnp.broadcast_to(acs_t[hl:hl + 1, :], (CHUNK, CHUNK))
                lm = jnp.exp(jnp.where(mask, colb - rowb, -1e30))
                m = cb * lm * jnp.broadcast_to(dt_t[hl:hl + 1, :], (CHUNK, CHUNK))
                bw = bg_t * jnp.broadcast_to(w2_t[hl:hl + 1, :], (CHUNK, CHUNK))
                mats.append((colb, m.astype(BF16), bw.astype(BF16)))
            (colb0, m0, bw0), (colb1, m1, bw1) = mats
            lhs = jnp.concatenate([jnp.concatenate([m0, m1], axis=1),
                                   jnp.concatenate([bw0, bw1], axis=1)], axis=0)
            xp = xs_bf[:, lanes]
            zero = jnp.zeros_like(xp)
            rhs = jnp.concatenate([jnp.where(lane_lo, xp, zero),
                                   jnp.where(lane_lo, zero, xp)], axis=0)
            out = _dot(lhs, rhs)
            ep = jnp.exp(jnp.where(lane_lo, colb0, colb1))
            y = out[:CHUNK] + yo[:, pp * LANES:(pp + 1) * LANES] * ep
            if not reverse:
                y = y + xs[:, lanes] * drow[:, lanes]
            pieces.append(y)
            dec = ep[0:1, :] if reverse else ep[CHUNK - 1:CHUNK, :]
            h_ref[:, lanes] = h_ref[:, lanes] * dec + out[CHUNK:]
    return pieces


def _ssd_fwd_kernel(x_ref, xp_ref, xn_ref, dt_ref, convw_ref, convb_ref, alog_ref, drow_ref,
                    y_ref, act_ref, cbuf, h_ref, *, tb, first_blocks, last_blocks):
    i = pl.program_id(0)
    is_first = _any_eq(i, first_blocks)
    is_last = _any_eq(i, last_blocks)
    cbuf[0:HALO, :] = jnp.where(is_first, 0.0, xp_ref[...].astype(F32))
    cbuf[HALO:HALO + tb, :] = x_ref[...].astype(F32)
    cbuf[HALO + tb:, :] = jnp.where(is_last, 0.0, xn_ref[...].astype(F32))

    @pl.when(is_first)
    def _():
        h_ref[...] = jnp.zeros_like(h_ref)

    a_row = -jnp.exp(alog_ref[...])
    drow = drow_ref[...]

    def body(c, carry):
        r0 = pl.multiple_of(c * CHUNK, CHUNK)
        win = cbuf[pl.ds(r0, CHUNK + 2 * HALO), :]
        acc = jnp.broadcast_to(convb_ref[...], (CHUNK, D_XBC))
        for k in range(SSD_CONV):
            off = HALO - SSD_CONV // 2 + k
            acc = acc + win[off:off + CHUNK, :] * convw_ref[k:k + 1, :]
        act = acc * _sigmoid(acc)
        act_ref[pl.ds(r0, CHUNK), :] = act.astype(act_ref.dtype)
        pieces = _ssd_chunk(act, dt_ref[pl.ds(r0, CHUNK), :], a_row, h_ref, drow, False)
        for p, y in enumerate(pieces):
            y_ref[pl.ds(r0, CHUNK), p * LANES:(p + 1) * LANES] = y.astype(y_ref.dtype)
        return carry

    lax.fori_loop(0, tb // CHUNK, body, 0)


def _ssd_bwd_kernel(act_ref, dt_ref, alog_ref, y_ref, h_ref, *, tb, last_blocks):
    nb = pl.num_programs(0)
    j = nb - 1 - pl.program_id(0)

    @pl.when(_any_eq(j, last_blocks))
    def _():
        h_ref[...] = jnp.zeros_like(h_ref)

    a_row = -jnp.exp(alog_ref[...])
    nchunk = tb // CHUNK

    def body(ci, carry):
        r0 = pl.multiple_of((nchunk - 1 - ci) * CHUNK, CHUNK)
        act = act_ref[pl.ds(r0, CHUNK), :].astype(F32)
        pieces = _ssd_chunk(act, dt_ref[pl.ds(r0, CHUNK), :], a_row, h_ref, None, True)
        for p, y in enumerate(pieces):
            y_ref[pl.ds(r0, CHUNK), p * LANES:(p + 1) * LANES] = y.astype(y_ref.dtype)
        return carry

    lax.fori_loop(0, nchunk, body, 0)


def _ssd_fwd(xbc, dt, convw, convb, alog, drow, tb, first_blocks, last_blocks):
    t = xbc.shape[0]
    hb = tb // HALO
    nh = t // HALO
    const = lambda i: (0, 0)
    row = lambda i: (i, 0)
    prev = lambda i: (jnp.maximum(i * hb - 1, 0), 0)
    nxt = lambda i: (jnp.minimum((i + 1) * hb, nh - 1), 0)
    kern = functools.partial(_ssd_fwd_kernel, tb=tb, first_blocks=first_blocks,
                             last_blocks=last_blocks)
    return pl.pallas_call(
        kern,
        grid=(t // tb,),
        in_specs=[
            pl.BlockSpec((tb, D_XBC), row),
            pl.BlockSpec((HALO, D_XBC), prev),
            pl.BlockSpec((HALO, D_XBC), nxt),
            pl.BlockSpec((tb, DT_PAD), row),
            pl.BlockSpec((8, D_XBC), const),
            pl.BlockSpec((1, D_XBC), const),
            pl.BlockSpec((1, DT_PAD), const),
            pl.BlockSpec((1, D_SSD), const),
        ],
        out_specs=[pl.BlockSpec((tb, D_SSD), row), pl.BlockSpec((tb, D_XBC), row)],
        out_shape=[jax.ShapeDtypeStruct((t, D_SSD), BF16), jax.ShapeDtypeStruct((t, D_XBC), BF16)],
        scratch_shapes=[
            pltpu.VMEM((tb + 2 * HALO, D_XBC), F32),
            pltpu.VMEM((D_STATE, D_SSD), F32),
        ],
        compiler_params=pltpu.CompilerParams(
            dimension_semantics=("arbitrary",), vmem_limit_bytes=VMEM_LIMIT),
        name="ssd_fwd",
    )(xbc, xbc, xbc, dt, convw, convb, alog, drow)


def _ssd_bwd(act, dt, alog, tb, last_blocks):
    t = act.shape[0]
    nb = t // tb
    const = lambda i: (0, 0)
    rev = lambda i: (nb - 1 - i, 0)
    kern = functools.partial(_ssd_bwd_kernel, tb=tb, last_blocks=last_blocks)
    return pl.pallas_call(
        kern,
        grid=(nb,),
        in_specs=[
            pl.BlockSpec((tb, D_XBC), rev),
            pl.BlockSpec((tb, DT_PAD), rev),
            pl.BlockSpec((1, DT_PAD), const),
        ],
        out_specs=pl.BlockSpec((tb, D_SSD), rev),
        out_shape=jax.ShapeDtypeStruct((t, D_SSD), BF16),
        scratch_shapes=[pltpu.VMEM((D_STATE, D_SSD), F32)],
        compiler_params=pltpu.CompilerParams(
            dimension_semantics=("arbitrary",), vmem_limit_bytes=VMEM_LIMIT),
        name="ssd_bwd",
    )(act, dt, alog)


CONV_ROWS = 32


def _mix_kernel(x_ref, yf_ref, yb_ref, z_ref, h_ref, hp_ref, hn_ref,
                gnw_ref, dww_ref, dwb_ref, lnw_ref, lnb_ref, wout_ref, n2w_ref,
                w1_ref, w2_ref, fnw_ref, o_ref, hbuf, cmbuf,
                *, tm, first_blocks, last_blocks, final):
    i = pl.program_id(0)
    is_first = _any_eq(i, first_blocks)
    is_last = _any_eq(i, last_blocks)

    zf = z_ref[...].astype(F32)
    y = (yf_ref[...].astype(F32) + yb_ref[...].astype(F32)) * (zf * _sigmoid(zf))
    gw = D_SSD // N_GROUPS
    parts = []
    for g in range(N_GROUPS):
        yg = y[:, g * gw:(g + 1) * gw]
        ms = jnp.mean(yg * yg, axis=-1, keepdims=True)
        parts.append(yg * lax.rsqrt(ms + EPS))
    y_ssd = (jnp.concatenate(parts, axis=1) * gnw_ref[...]).astype(BF16)

    hbuf[0:HALO, :] = jnp.where(is_first, 0.0, hp_ref[...].astype(F32))
    hbuf[HALO:HALO + tm, :] = h_ref[...].astype(F32)
    hbuf[HALO + tm:, :] = jnp.where(is_last, 0.0, hn_ref[...].astype(F32))

    def conv_body(s, carry):
        r0 = pl.multiple_of(s * CONV_ROWS, CONV_ROWS)
        win = hbuf[pl.ds(r0, CONV_ROWS + 2 * HALO), :]
        acc = jnp.broadcast_to(dwb_ref[...], (CONV_ROWS, D_CM))
        for k in range(CM_CONV):
            off = HALO - CM_CONV // 2 + k
            acc = acc + win[off:off + CONV_ROWS, :] * dww_ref[k:k + 1, :]
        cmbuf[pl.ds(r0, CONV_ROWS), :] = acc
        return carry

    lax.fori_loop(0, tm // CONV_ROWS, conv_body, 0)
    cm = cmbuf[...]
    mu = jnp.mean(cm, axis=-1, keepdims=True)
    xc = cm - mu
    var = jnp.mean(xc * xc, axis=-1, keepdims=True)
    ln = xc * lax.rsqrt(var + EPS) * lnw_ref[...] + lnb_ref[...]
    y_cm = (ln * _sigmoid(ln)).astype(BF16)

    x1 = x_ref[...] + _dot(jnp.concatenate([y_ssd, y_cm], axis=1), wout_ref[...])

    ms2 = jnp.mean(x1 * x1, axis=-1, keepdims=True)
    n2 = (x1 * lax.rsqrt(ms2 + EPS) * n2w_ref[...]).astype(BF16)
    hmid = jnp.maximum(_dot(n2, w1_ref[...]), 0.0)
    acc = x1 + _dot((hmid * hmid).astype(BF16), w2_ref[...])
    if final:
        msf = jnp.mean(acc * acc, axis=-1, keepdims=True)
        acc = acc * lax.rsqrt(msf + EPS) * fnw_ref[...]
    o_ref[...] = acc


def _mix(x, yf, yb, z, h, gnw, dww, dwb, lnw, lnb, wout, n2w, w1, w2, fnw,
         tm, first_blocks, last_blocks, final):
    t = x.shape[0]
    hb = tm // HALO
    nh = t // HALO
    const = lambda i: (0, 0)
    row = lambda i: (i, 0)
    prev = lambda i: (jnp.maximum(i * hb - 1, 0), 0)
    nxt = lambda i: (jnp.minimum((i + 1) * hb, nh - 1), 0)
    once = pl.Buffered(1)
    kern = functools.partial(_mix_kernel, tm=tm, first_blocks=first_blocks,
                             last_blocks=last_blocks, final=final)
    return pl.pallas_call(
        kern,
        grid=(t // tm,),
        in_specs=[
            pl.BlockSpec((tm, D_MODEL), row),
            pl.BlockSpec((tm, D_SSD), row),
            pl.BlockSpec((tm, D_SSD), row),
            pl.BlockSpec((tm, D_SSD), row),
            pl.BlockSpec((tm, D_CM), row),
            pl.BlockSpec((HALO, D_CM), prev),
            pl.BlockSpec((HALO, D_CM), nxt),
            pl.BlockSpec((1, D_SSD), const),
            pl.BlockSpec((32, D_CM), const),
            pl.BlockSpec((1, D_CM), const),
            pl.BlockSpec((1, D_CM), const),
            pl.BlockSpec((1, D_CM), const),
            pl.BlockSpec((D_SSD + D_CM, D_MODEL), const, pipeline_mode=once),
            pl.BlockSpec((1, D_MODEL), const),
            pl.BlockSpec((D_MODEL, D_FF), const, pipeline_mode=once),
            pl.BlockSpec((D_FF, D_MODEL), const, pipeline_mode=once),
            pl.BlockSpec((1, D_MODEL), const),
        ],
        out_specs=pl.BlockSpec((tm, D_MODEL), row),
        out_shape=jax.ShapeDtypeStruct((t, D_MODEL), F32),
        scratch_shapes=[
            pltpu.VMEM((tm + 2 * HALO, D_CM), F32),
            pltpu.VMEM((tm, D_CM), F32),
        ],
        compiler_params=pltpu.CompilerParams(
            dimension_semantics=("arbitrary",), vmem_limit_bytes=VMEM_LIMIT),
        name="mix_mlp",
    )(x, yf, yb, z, h, h, h, gnw, dww, dwb, lnw, lnb, wout, n2w, w1, w2, fnw)


def _block_size(seq_lens, want):
    b = want
    while any(s % b for s in seq_lens):
        b //= 2
    assert b >= CHUNK, "sequence lengths must be multiples of the SSD chunk"
    return b


def kernel(x_prompt, x_sample, norm1_w, w_in, ssd_conv_w, ssd_conv_b, dt_bias, a_log, d_skip,
           ssd_norm_w, cm_dw_w, cm_dw_b, cm_ln_w, cm_ln_b, w_out, norm2_w, w_mlp_in, w_mlp_out,
           final_norm_w):
    depth = w_in.shape[0]
    seq_lens = [x_prompt.shape[1]] * x_prompt.shape[0] + [x_sample.shape[1]] * x_sample.shape[0]
    n_prompt = x_prompt.shape[0] * x_prompt.shape[1]
    x = jnp.concatenate([x_prompt.reshape(-1, D_MODEL), x_sample.reshape(-1, D_MODEL)], axis=0)

    tb = _block_size(seq_lens, 512)
    starts = np.concatenate([[0], np.cumsum(seq_lens)[:-1]])
    ends = np.cumsum(seq_lens)
    first_blocks = tuple(int(s) // tb for s in starts)
    last_blocks = tuple(int(e) // tb - 1 for e in ends)

    s1 = D_SSD
    s2 = s1 + D_XBC
    s3 = s2 + 2 * N_HEADS
    row = lambda v: v.reshape(1, -1).astype(F32)
    for l in range(depth):
        wl = w_in[l]
        wz = wl[:, :s1].astype(BF16)
        wx = jnp.concatenate([wl[:, s1:s3], jnp.zeros((D_MODEL, DT_PAD - 2 * N_HEADS), F32)],
                             axis=1).astype(BF16)
        wcm = wl[:, s3:].astype(BF16)
        dtb = jnp.pad(dt_bias[l].reshape(1, -1), ((0, 0), (0, DT_PAD - 2 * N_HEADS)))
        alog = jnp.pad(a_log[l].reshape(1, -1), ((0, 0), (0, DT_PAD - 2 * N_HEADS)))
        z, xbc, dt, h = _inproj(x, row(norm1_w[l]), wz, wx, wcm, dtb, tb)

        convw = jnp.pad(ssd_conv_w[l], ((0, 8 - SSD_CONV), (0, 0)))
        drow = jnp.repeat(d_skip[l], HEAD_DIM).reshape(1, -1)
        yf, act = _ssd_fwd(xbc, dt, convw, row(ssd_conv_b[l]), alog, drow, tb,
                           first_blocks, last_blocks)
        yb = _ssd_bwd(act, dt, alog, tb, last_blocks)

        dww = jnp.pad(cm_dw_w[l], ((0, 32 - CM_CONV), (0, 0)))
        x = _mix(x, yf, yb, z, h, row(ssd_norm_w[l]), dww, row(cm_dw_b[l]), row(cm_ln_w[l]),
                 row(cm_ln_b[l]), w_out[l].astype(BF16), row(norm2_w[l]),
                 w_mlp_in[l].astype(BF16), w_mlp_out[l].astype(BF16), row(final_norm_w),
                 tb, first_blocks, last_blocks, l == depth - 1)

    y_prompt = x[:n_prompt].reshape(x_prompt.shape)
    y_sample = x[n_prompt:].reshape(x_sample.shape)
    return y_prompt, y_sample
```

```python
import functools

import jax
import jax.numpy as jnp
import numpy as np
from jax import lax
from jax.experimental import pallas as pl
from jax.experimental.pallas import tpu as pltpu

D_MODEL = 1024
D_SSD = 1024
HEAD_DIM = 64
N_HEADS = 16
N_GROUPS = 2
D_STATE = 128
CHUNK = 128
D_XBC = D_SSD + 2 * N_GROUPS * D_STATE
D_CM = 1024
SSD_CONV = 5
CM_CONV = 31
D_FF = 4096
EPS = 1e-5

LANES = 128
SUBLANES = 8
HALO = 16
DT_PAD = LANES
VMEM_LIMIT = 56 * 1024 * 1024

F32 = jnp.float32
BF16 = jnp.bfloat16


def _dot(a, b):
    return jnp.dot(a, b, preferred_element_type=F32)


def _sigmoid(x):
    return 1.0 / (1.0 + jnp.exp(-x))


def _any_eq(j, values):
    return functools.reduce(jnp.logical_or, [j == v for v in values])


def _stream_specs(xs, tm, tile_of):
    specs, first = [], 0
    for a in xs:
        n = a.shape[0] // tm
        specs.append(pl.BlockSpec(
            (tm, D_MODEL), lambda i, first=first, n=n: (jnp.clip(tile_of(i) - first, 0, n - 1), 0)))
        first += n
    return specs


def _stream_tile(x_refs, ends, tile):
    x = x_refs[-1][...]
    for ref, end in reversed(list(zip(x_refs[:-1], ends[:-1]))):
        x = jnp.where(tile < end, ref[...], x)
    return x


def _stream_ends(xs, tm):
    return tuple(int(v) for v in np.cumsum([a.shape[0] // tm for a in xs]))


def _inproj_kernel(*refs, ends):
    nx = len(ends)
    nw_ref, wz_ref, wx_ref, wcm_ref, dtb_ref, z_ref, xbc_ref, dt_ref, h_ref = refs[nx:]
    x = _stream_tile(refs[:nx], ends, pl.program_id(0))
    ms = jnp.mean(x * x, axis=-1, keepdims=True)
    n = (x * lax.rsqrt(ms + EPS) * nw_ref[...]).astype(BF16)
    z_ref[...] = _dot(n, wz_ref[...]).astype(BF16)
    xd = _dot(n, wx_ref[...])
    xbc_ref[...] = xd[:, :D_XBC].astype(BF16)
    v = xd[:, D_XBC:] + dtb_ref[...]
    dt_ref[...] = jnp.maximum(v, 0.0) + jnp.log(1.0 + jnp.exp(-jnp.abs(v)))
    u = _dot(n, wcm_ref[...])
    h_ref[...] = (u[:, :D_CM] * _sigmoid(u[:, D_CM:])).astype(BF16)


def _inproj(xs, nw, wz, wx, wcm, dtb, tm):
    t = sum(a.shape[0] for a in xs)
    const = lambda i: (0, 0)
    row = lambda i: (i, 0)
    return pl.pallas_call(
        functools.partial(_inproj_kernel, ends=_stream_ends(xs, tm)),
        grid=(t // tm,),
        in_specs=_stream_specs(xs, tm, lambda i: i) + [
            pl.BlockSpec((1, D_MODEL), const),
            pl.BlockSpec((D_MODEL, D_SSD), const),
            pl.BlockSpec((D_MODEL, D_XBC + DT_PAD), const),
            pl.BlockSpec((D_MODEL, 2 * D_CM), const),
            pl.BlockSpec((1, DT_PAD), const),
        ],
        out_specs=[
            pl.BlockSpec((tm, D_SSD), row),
            pl.BlockSpec((tm, D_XBC), row),
            pl.BlockSpec((tm, DT_PAD), row),
            pl.BlockSpec((tm, D_CM), row),
        ],
        out_shape=[
            jax.ShapeDtypeStruct((t, D_SSD), BF16),
            jax.ShapeDtypeStruct((t, D_XBC), BF16),
            jax.ShapeDtypeStruct((t, DT_PAD), F32),
            jax.ShapeDtypeStruct((t, D_CM), BF16),
        ],
        compiler_params=pltpu.CompilerParams(
            dimension_semantics=("arbitrary",), vmem_limit_bytes=VMEM_LIMIT),
        name="inproj",
    )(*xs, nw, wz, wx, wcm, dtb)


SSD_CONV_ROWS = 64


def _pair_words(win, pbuf, nrow):
    nword = nrow // 2
    for c in range(win.shape[1] // LANES):
        even = pltpu.bitcast(win[:, c * LANES:(c + 1) * LANES], jnp.uint32)
        nxt = pltpu.roll(even, nword - 1, axis=0)
        odd = (even >> 16) | (nxt << 16)
        pbuf[c, pl.ds(0, nword, stride=2), :] = even
        pbuf[c, pl.ds(1, nword, stride=2), :] = odd


def _scan_mask(reverse):
    li = lax.broadcasted_iota(jnp.int32, (CHUNK, CHUNK), 0)
    si = lax.broadcasted_iota(jnp.int32, (CHUNK, CHUNK), 1)
    return (si >= li) if reverse else (si <= li)


def _ssd_prepare(c, act, dt, a_row, sc, reverse):
    acs_s, acst_s, dtt_s, w2t_s, bgt_s, cb_s = sc
    a = dt * a_row
    tmat = jnp.where(_scan_mask(reverse), 1.0, 0.0).astype(BF16)
    a1 = a.astype(BF16)
    rem = a - a1.astype(F32)
    a2 = rem.astype(BF16)
    a3 = (rem - a2.astype(F32)).astype(BF16)
    parts = _dot(tmat, jnp.concatenate([a1, a2, a3], axis=1))
    acs = parts[:, :DT_PAD] + parts[:, DT_PAD:2 * DT_PAD] + parts[:, 2 * DT_PAD:]
    tot = acs[0:1, :] if reverse else acs[CHUNK - 1:CHUNK, :]
    w2 = dt * jnp.exp(tot - acs)
    acs_s[c * CHUNK:(c + 1) * CHUNK, :] = acs
    acst_s[c] = acs.T
    dtt_s[c] = dt.T
    w2t_s[c] = w2.T
    for g in range(N_GROUPS):
        bg = act[:, D_SSD + g * D_STATE:D_SSD + (g + 1) * D_STATE]
        cg = act[:, D_SSD + N_GROUPS * D_STATE + g * D_STATE:
                 D_SSD + N_GROUPS * D_STATE + (g + 1) * D_STATE]
        cb_s[c, g] = lax.dot_general(cg.astype(BF16), bg.astype(BF16), (((1,), (1,)), ((), ())),
                                     preferred_element_type=F32)
        bgt_s[c, g] = bg.T


def _ssd_scan_chunk(c, act_ref, sc, h_ref, y_ref, drow, reverse):
    acs_s, acst_s, dtt_s, w2t_s, bgt_s, cb_s = sc
    rows = pl.ds(pl.multiple_of(c * CHUNK, CHUNK), CHUNK)
    acs = acs_s[rows, :]
    acs_t = acst_s[c]
    dt_t = dtt_s[c]
    w2_t = w2t_s[c]
    mask = _scan_mask(reverse)
    lane_lo = lax.broadcasted_iota(jnp.int32, (CHUNK, LANES), 1) < HEAD_DIM
    hoff = N_HEADS if reverse else 0
    heads_per_group = N_HEADS // N_GROUPS
    gw = heads_per_group * HEAD_DIM
    c0 = D_SSD + N_GROUPS * D_STATE
    for g in range(N_GROUPS):
        cb = cb_s[c, g]
        bg_t = bgt_s[c, g]
        cg_bf = act_ref[rows, c0 + g * D_STATE:c0 + (g + 1) * D_STATE]
        yo = _dot(cg_bf, h_ref[:, g * gw:(g + 1) * gw].astype(BF16))
        for pp in range(heads_per_group // 2):
            lanes = slice(g * gw + pp * LANES, g * gw + (pp + 1) * LANES)
            mats = []
            for hh in range(2):
                hl = hoff + g * heads_per_group + 2 * pp + hh
                colb = jnp.broadcast_to(acs[:, hl:hl + 1], (CHUNK, CHUNK))
                rowb = jnp.broadcast_to(acs_t[hl:hl + 1, :], (CHUNK, CHUNK))
                lm = jnp.exp(jnp.where(mask, colb - rowb, -1e30))
                m = cb * lm * jnp.broadcast_to(dt_t[hl:hl + 1, :], (CHUNK, CHUNK))
                bw = bg_t * jnp.broadcast_to(w2_t[hl:hl + 1, :], (CHUNK, CHUNK))
                mats.append((colb, m.astype(BF16), bw.astype(BF16)))
            (colb0, m0, bw0), (colb1, m1, bw1) = mats
            lhs = jnp.concatenate([jnp.concatenate([m0, m1], axis=1),
                                   jnp.concatenate([bw0, bw1], axis=1)], axis=0)
            xp = act_ref[rows, lanes]
            zero = jnp.zeros_like(xp)
            rhs = jnp.concatenate([jnp.where(lane_lo, xp, zero),
                                   jnp.where(lane_lo, zero, xp)], axis=0)
            out = _dot(lhs, rhs)
            ep = jnp.exp(jnp.where(lane_lo, colb0, colb1))
            y = out[:CHUNK] + yo[:, pp * LANES:(pp + 1) * LANES] * ep
            if not reverse:
                y = y + xp.astype(F32) * drow[:, lanes]
            y_ref[rows, lanes] = y.astype(y_ref.dtype)
            dec = ep[0:1, :] if reverse else ep[CHUNK - 1:CHUNK, :]
            h_ref[:, lanes] = h_ref[:, lanes] * dec + out[CHUNK:]


def _scan_scratch(tb):
    nchunk = tb // CHUNK
    return [
        pltpu.VMEM((tb, DT_PAD), F32),
        pltpu.VMEM((nchunk, DT_PAD, CHUNK), F32),
        pltpu.VMEM((nchunk, DT_PAD, CHUNK), F32),
        pltpu.VMEM((nchunk, DT_PAD, CHUNK), F32),
        pltpu.VMEM((nchunk, N_GROUPS, D_STATE, CHUNK), F32),
        pltpu.VMEM((nchunk, N_GROUPS, CHUNK, CHUNK), F32),
    ]


def _ssd_fwd_kernel(x_ref, xp_ref, xn_ref, dt_ref, convw_ref, convb_ref, alog_ref, drow_ref,
                    y_ref, act_ref, pbuf, cs, *rest, tb, first_blocks, last_blocks):
    sc, h_ref = rest[:-1], rest[-1]
    i = pl.program_id(0)
    is_first = _any_eq(i, first_blocks)
    is_last = _any_eq(i, last_blocks)
    ncol = D_XBC // LANES
    nchunk = tb // CHUNK
    zero = jnp.zeros((HALO, D_XBC), BF16)
    win = jnp.concatenate([jnp.where(is_first, zero, xp_ref[...]), x_ref[...],
                           jnp.where(is_last, zero, xn_ref[...])], axis=0)
    _pair_words(win, pbuf, tb + 2 * HALO)

    @pl.when(is_first)
    def _():
        h_ref[...] = jnp.zeros_like(h_ref)

    ngrp = SSD_CONV_ROWS // (2 * SUBLANES)

    def conv_body(j, carry):
        r0 = pl.multiple_of(j * SSD_CONV_ROWS, SSD_CONV_ROWS)
        for c in range(ncol):
            accs = [None] * ngrp
            for k in range(SSD_CONV):
                wk = convw_ref[k, c].astype(F32)
                for g in range(ngrp):
                    row = r0 + 2 * SUBLANES * g + HALO - SSD_CONV // 2 + k
                    words = pbuf[c, pl.ds(row, SUBLANES, stride=2), :]
                    term = pltpu.bitcast(words, BF16).astype(F32) * wk
                    accs[g] = term if accs[g] is None else accs[g] + term
            for g in range(ngrp):
                cs[c, pl.ds(r0 + 2 * SUBLANES * g, 2 * SUBLANES), :] = accs[g]
        return carry

    lax.fori_loop(0, tb // SSD_CONV_ROWS, conv_body, 0)

    a_row = -jnp.exp(alog_ref[...])
    for c in range(nchunk):
        rows = slice(c * CHUNK, (c + 1) * CHUNK)
        pre = jnp.concatenate([cs[ct, rows, :] for ct in range(ncol)], axis=1) + convb_ref[0:1, :]
        act = pre * _sigmoid(pre)
        act_ref[rows, :] = act.astype(act_ref.dtype)
        _ssd_prepare(c, act, dt_ref[rows, :], a_row, sc, False)

    drow = drow_ref[...]

    def body(c, carry):
        _ssd_scan_chunk(c, act_ref, sc, h_ref, y_ref, drow, False)
        return carry

    lax.fori_loop(0, nchunk, body, 0, unroll=True)


def _ssd_bwd_kernel(act_ref, dt_ref, alog_ref, y_ref, *rest, tb, last_blocks):
    sc, h_ref = rest[:-1], rest[-1]
    nb = pl.num_programs(0)
    j = nb - 1 - pl.program_id(0)

    @pl.when(_any_eq(j, last_blocks))
    def _():
        h_ref[...] = jnp.zeros_like(h_ref)

    a_row = -jnp.exp(alog_ref[...])
    nchunk = tb // CHUNK
    for c in range(nchunk):
        rows = slice(c * CHUNK, (c + 1) * CHUNK)
        _ssd_prepare(c, act_ref[rows, :].astype(F32), dt_ref[rows, :], a_row, sc, True)

    def body(ci, carry):
        _ssd_scan_chunk(nchunk - 1 - ci, act_ref, sc, h_ref, y_ref, None, True)
        return carry

    lax.fori_loop(0, nchunk, body, 0, unroll=True)


def _ssd_fwd(xbc, dt, convw, convb, alog, drow, tb, first_blocks, last_blocks):
    t = xbc.shape[0]
    hb = tb // HALO
    nh = t // HALO
    const = lambda i: (0, 0)
    row = lambda i: (i, 0)
    prev = lambda i: (jnp.maximum(i * hb - 1, 0), 0)
    nxt = lambda i: (jnp.minimum((i + 1) * hb, nh - 1), 0)
    kern = functools.partial(_ssd_fwd_kernel, tb=tb, first_blocks=first_blocks,
                             last_blocks=last_blocks)
    return pl.pallas_call(
        kern,
        grid=(t // tb,),
        in_specs=[
            pl.BlockSpec((tb, D_XBC), row),
            pl.BlockSpec((HALO, D_XBC), prev),
            pl.BlockSpec((HALO, D_XBC), nxt),
            pl.BlockSpec((tb, DT_PAD), row),
            pl.BlockSpec((SSD_CONV, D_XBC // LANES, 2 * SUBLANES, LANES), lambda i: (0, 0, 0, 0)),
            pl.BlockSpec((SUBLANES, D_XBC), const),
            pl.BlockSpec((1, DT_PAD), const),
            pl.BlockSpec((1, D_SSD), const),
        ],
        out_specs=[pl.BlockSpec((tb, D_SSD), row), pl.BlockSpec((tb, D_XBC), row)],
        out_shape=[jax.ShapeDtypeStruct((t, D_SSD), BF16), jax.ShapeDtypeStruct((t, D_XBC), BF16)],
        scratch_shapes=[
            pltpu.VMEM((D_XBC // LANES, tb + 2 * HALO, LANES), jnp.uint32),
            pltpu.VMEM((D_XBC // LANES, tb, LANES), F32),
        ] + _scan_scratch(tb) + [pltpu.VMEM((D_STATE, D_SSD), F32)],
        compiler_params=pltpu.CompilerParams(
            dimension_semantics=("arbitrary",), vmem_limit_bytes=VMEM_LIMIT),
        name="ssd_fwd",
    )(xbc, xbc, xbc, dt, convw, convb, alog, drow)


def _ssd_bwd(act, dt, alog, tb, last_blocks):
    t = act.shape[0]
    nb = t // tb
    const = lambda i: (0, 0)
    rev = lambda i: (nb - 1 - i, 0)
    kern = functools.partial(_ssd_bwd_kernel, tb=tb, last_blocks=last_blocks)
    return pl.pallas_call(
        kern,
        grid=(nb,),
        in_specs=[
            pl.BlockSpec((tb, D_XBC), rev),
            pl.BlockSpec((tb, DT_PAD), rev),
            pl.BlockSpec((1, DT_PAD), const),
        ],
        out_specs=pl.BlockSpec((tb, D_SSD), rev),
        out_shape=jax.ShapeDtypeStruct((t, D_SSD), BF16),
        scratch_shapes=_scan_scratch(tb) + [pltpu.VMEM((D_STATE, D_SSD), F32)],
        compiler_params=pltpu.CompilerParams(
            dimension_semantics=("arbitrary",), vmem_limit_bytes=VMEM_LIMIT),
        name="ssd_bwd",
    )(act, dt, alog)


CONV_ROWS = 64


def _conv31_rows(pbuf, dww_ref, cm_ref, r0):
    ngrp = CONV_ROWS // (2 * SUBLANES)
    for c in range(D_CM // LANES):
        accs = [None] * ngrp
        for k in range(CM_CONV):
            wk = dww_ref[k, c].astype(F32)
            for g in range(ngrp):
                words = pbuf[c, pl.ds(r0 + 2 * SUBLANES * g + k + 1, SUBLANES, stride=2), :]
                term = pltpu.bitcast(words, BF16).astype(F32) * wk
                accs[g] = term if accs[g] is None else accs[g] + term
        for g in range(ngrp):
            cm_ref[c, pl.ds(r0 + 2 * SUBLANES * g, 2 * SUBLANES), :] = accs[g]


def _mix_kernel(*refs, tm, nb, ends, out_ends, first_blocks, last_blocks, final):
    nx, no = len(ends), len(out_ends)
    (yf_ref, yb_ref, z_ref, h_ref, hp_ref, hn_ref, gnw_ref, dww_ref, dwb_ref, lnw_ref, lnb_ref,
     wout_ref, n2w_ref, w1_ref, w2_ref, fnw_ref) = refs[nx:nx + 16]
    o_refs = refs[nx + 16:nx + 16 + no]
    pbuf, cm_ref, n2_ref, acc_ref, hm_ref = refs[nx + 16 + no:]
    i = pl.program_id(0)
    a = jnp.minimum(i, nb - 1)
    is_first = _any_eq(a, first_blocks)
    is_last = _any_eq(a, last_blocks)

    @pl.when(i == 0)
    def _():
        cm_ref[...] = jnp.zeros_like(cm_ref)

    zf = z_ref[...].astype(F32)
    y = (yf_ref[...].astype(F32) + yb_ref[...].astype(F32)) * (zf * _sigmoid(zf))
    gw = D_SSD // N_GROUPS
    parts = []
    for g in range(N_GROUPS):
        yg = y[:, g * gw:(g + 1) * gw]
        ms = jnp.mean(yg * yg, axis=-1, keepdims=True)
        parts.append(yg * lax.rsqrt(ms + EPS))
    y_ssd = (jnp.concatenate(parts, axis=1) * gnw_ref[...]).astype(BF16)

    ncol = D_CM // LANES
    x = _stream_tile(refs[:nx], ends, jnp.maximum(i - 1, 0))
    acc_ref[...] = x + _dot(y_ssd, wout_ref[0:D_SSD, :])
    cm = jnp.concatenate([cm_ref[c] for c in range(ncol)], axis=1) + dwb_ref[0:1, :]
    mu = jnp.mean(cm, axis=-1, keepdims=True)
    xc = cm - mu
    var = jnp.mean(xc * xc, axis=-1, keepdims=True)
    ln = xc * lax.rsqrt(var + EPS) * lnw_ref[0:1, :] + lnb_ref[0:1, :]
    y_cm = (ln * _sigmoid(ln)).astype(BF16)
    acc_ref[...] += _dot(y_cm, wout_ref[D_SSD:, :])
    x1 = acc_ref[...]

    ms2 = jnp.mean(x1 * x1, axis=-1, keepdims=True)
    n2_ref[...] = (x1 * lax.rsqrt(ms2 + EPS) * n2w_ref[...]).astype(BF16)

    zero = jnp.zeros((HALO, D_CM), BF16)
    win = jnp.concatenate([jnp.where(is_first, zero, hp_ref[...]), h_ref[...],
                           jnp.where(is_last, zero, hn_ref[...])], axis=0)
    _pair_words(win, pbuf, tm + 2 * HALO)

    fs = w1_ref.shape[1]
    nslice = tm // CONV_ROWS

    def up(j):
        w1j = w1_ref[pl.ds(pl.multiple_of(j * D_MODEL, D_MODEL), D_MODEL), :]
        hmid = jnp.maximum(_dot(n2_ref[...], w1j), 0.0)
        return (hmid * hmid).astype(BF16)

    def down(j, slot):
        w2j = w2_ref[pl.ds(pl.multiple_of(j * fs, fs), fs), :]
        acc_ref[...] += _dot(hm_ref[slot], w2j)

    def conv(j):
        _conv31_rows(pbuf, dww_ref, cm_ref, pl.multiple_of(j * CONV_ROWS, CONV_ROWS))

    hm_ref[0] = up(0)

    def body(jj, carry):
        j = 2 * jj
        hm_ref[1] = up(j + 1)
        down(j, 0)
        conv(j)
        hm_ref[0] = up(j + 2)
        down(j + 1, 1)
        conv(j + 1)
        return carry

    lax.fori_loop(0, nslice // 2 - 1, body, 0)
    hm_ref[1] = up(nslice - 1)
    down(nslice - 2, 0)
    conv(nslice - 2)
    down(nslice - 1, 1)
    conv(nslice - 1)
    acc = acc_ref[...]
    if final:
        msf = jnp.mean(acc * acc, axis=-1, keepdims=True)
        acc = acc * lax.rsqrt(msf + EPS) * fnw_ref[...]
    if no == 1:
        o_refs[0][...] = acc
    else:
        b = jnp.maximum(i - 1, 0)
        first = 0
        for o_ref, end in zip(o_refs, out_ends):
            @pl.when(jnp.logical_and(b >= first, b < end))
            def _(o_ref=o_ref):
                o_ref[...] = acc
            first = end


def _mix(xs, yf, yb, z, h, gnw, dww, dwb, lnw, lnb, wout, n2w, w1, w2, fnw,
         tm, first_blocks, last_blocks, final, out_rows):
    t = sum(a.shape[0] for a in xs)
    nb = t // tm
    outs = [jax.ShapeDtypeStruct((r, D_MODEL), F32) for r in out_rows]
    hb = tm // HALO
    nh = t // HALO
    const = lambda i: (0, 0)
    done = lambda i: (jnp.maximum(i - 1, 0), 0)
    nxt_tile = lambda i: (jnp.minimum(i, nb - 1), 0)
    prev = lambda i: (jnp.maximum(jnp.minimum(i, nb - 1) * hb - 1, 0), 0)
    nxt = lambda i: (jnp.minimum((jnp.minimum(i, nb - 1) + 1) * hb, nh - 1), 0)
    once = pl.Buffered(1)
    nslice = tm // CONV_ROWS
    fs = D_FF // nslice
    w1 = w1.reshape(D_MODEL, nslice, fs).transpose(1, 0, 2).reshape(nslice * D_MODEL, fs)
    kern = functools.partial(_mix_kernel, tm=tm, nb=nb, ends=_stream_ends(xs, tm),
                             out_ends=_stream_ends(outs, tm), first_blocks=first_blocks,
                             last_blocks=last_blocks, final=final)
    return pl.pallas_call(
        kern,
        grid=(nb + 1,),
        in_specs=_stream_specs(xs, tm, lambda i: jnp.maximum(i - 1, 0)) + [
            pl.BlockSpec((tm, D_SSD), done),
            pl.BlockSpec((tm, D_SSD), done),
            pl.BlockSpec((tm, D_SSD), done),
            pl.BlockSpec((tm, D_CM), nxt_tile),
            pl.BlockSpec((HALO, D_CM), prev),
            pl.BlockSpec((HALO, D_CM), nxt),
            pl.BlockSpec((1, D_SSD), const),
            pl.BlockSpec((CM_CONV, D_CM // LANES, 2 * SUBLANES, LANES), lambda i: (0, 0, 0, 0)),
            pl.BlockSpec((SUBLANES, D_CM), const),
            pl.BlockSpec((SUBLANES, D_CM), const),
            pl.BlockSpec((SUBLANES, D_CM), const),
            pl.BlockSpec((D_SSD + D_CM, D_MODEL), const, pipeline_mode=once),
            pl.BlockSpec((1, D_MODEL), const),
            pl.BlockSpec((nslice * D_MODEL, fs), const, pipeline_mode=once),
            pl.BlockSpec((D_FF, D_MODEL), const, pipeline_mode=once),
            pl.BlockSpec((1, D_MODEL), const),
        ],
        out_specs=_stream_specs(outs, tm, lambda i: jnp.maximum(i - 1, 0)),
        out_shape=outs,
        scratch_shapes=[
            pltpu.VMEM((D_CM // LANES, tm + 2 * HALO, LANES), jnp.uint32),
            pltpu.VMEM((D_CM // LANES, tm, LANES), F32),
            pltpu.VMEM((tm, D_MODEL), BF16),
            pltpu.VMEM((tm, D_MODEL), F32),
            pltpu.VMEM((2, tm, fs), BF16),
        ],
        compiler_params=pltpu.CompilerParams(
            dimension_semantics=("arbitrary",), vmem_limit_bytes=VMEM_LIMIT),
        name="mix_mlp",
    )(*xs, yf, yb, z, h, h, h, gnw, dww, dwb, lnw, lnb, wout, n2w, w1, w2, fnw)


def _block_size(seq_lens, want):
    b = want
    while any(s % b for s in seq_lens):
        b //= 2
    assert b >= CHUNK, "sequence lengths must be multiples of the SSD chunk"
    return b


def kernel(x_prompt, x_sample, norm1_w, w_in, ssd_conv_w, ssd_conv_b, dt_bias, a_log, d_skip,
           ssd_norm_w, cm_dw_w, cm_dw_b, cm_ln_w, cm_ln_b, w_out, norm2_w, w_mlp_in, w_mlp_out,
           final_norm_w):
    depth = w_in.shape[0]
    seq_lens = [x_prompt.shape[1]] * x_prompt.shape[0] + [x_sample.shape[1]] * x_sample.shape[0]
    n_prompt = x_prompt.shape[0] * x_prompt.shape[1]
    xs = (x_prompt.reshape(-1, D_MODEL), x_sample.reshape(-1, D_MODEL))

    tb = _block_size(seq_lens, 512)
    starts = np.concatenate([[0], np.cumsum(seq_lens)[:-1]])
    ends = np.cumsum(seq_lens)
    first_blocks = tuple(int(s) // tb for s in starts)
    last_blocks = tuple(int(e) // tb - 1 for e in ends)

    s1 = D_SSD
    s2 = s1 + D_XBC
    s3 = s2 + 2 * N_HEADS
    row = lambda v: v.reshape(1, -1).astype(F32)
    for l in range(depth):
        wl = w_in[l]
        wz = wl[:, :s1].astype(BF16)
        wx = jnp.concatenate([wl[:, s1:s3], jnp.zeros((D_MODEL, DT_PAD - 2 * N_HEADS), F32)],
                             axis=1).astype(BF16)
        wcm = wl[:, s3:].astype(BF16)
        dtb = jnp.pad(dt_bias[l].reshape(1, -1), ((0, 0), (0, DT_PAD - 2 * N_HEADS)))
        alog = jnp.pad(a_log[l].reshape(1, -1), ((0, 0), (0, DT_PAD - 2 * N_HEADS)))
        z, xbc, dt, h = _inproj(xs, row(norm1_w[l]), wz, wx, wcm, dtb, tb)

        rep = lambda v: jnp.broadcast_to(v.reshape(1, -1).astype(F32), (SUBLANES, v.size))
        convw = jnp.broadcast_to(
            ssd_conv_w[l].astype(BF16).reshape(SSD_CONV, D_XBC // LANES, 1, LANES),
            (SSD_CONV, D_XBC // LANES, 2 * SUBLANES, LANES))
        drow = jnp.repeat(d_skip[l], HEAD_DIM).reshape(1, -1)
        yf, act = _ssd_fwd(xbc, dt, convw, rep(ssd_conv_b[l]), alog, drow, tb,
                           first_blocks, last_blocks)
        yb = _ssd_bwd(act, dt, alog, tb, last_blocks)

        dww = jnp.broadcast_to(cm_dw_w[l].astype(BF16).reshape(CM_CONV, D_CM // LANES, 1, LANES),
                               (CM_CONV, D_CM // LANES, 2 * SUBLANES, LANES))
        last = l == depth - 1
        n_sample = x_sample.shape[0] * x_sample.shape[1]
        out_rows = (n_prompt, n_sample) if last else (n_prompt + n_sample,)
        x = _mix(xs, yf, yb, z, h, row(ssd_norm_w[l]), dww, rep(cm_dw_b[l]), rep(cm_ln_w[l]),
                 rep(cm_ln_b[l]), w_out[l].astype(BF16), row(norm2_w[l]),
                 w_mlp_in[l].astype(BF16), w_mlp_out[l].astype(BF16), row(final_norm_w),
                 tb, first_blocks, last_blocks, last, out_rows)
        xs = tuple(x)

    y_prompt, y_sample = xs
    return y_prompt.reshape(x_prompt.shape), y_sample.reshape(x_sample.shape)
```

```python
import functools

import jax
import jax.numpy as jnp
import numpy as np
from jax import lax
from jax.experimental import pallas as pl
from jax.experimental.pallas import tpu as pltpu

D_MODEL = 1024
D_SSD = 1024
HEAD_DIM = 64
N_HEADS = 16
N_GROUPS = 2
D_STATE = 128
CHUNK = 128
D_XBC = D_SSD + 2 * N_GROUPS * D_STATE
D_CM = 1024
SSD_CONV = 5
CM_CONV = 31
D_FF = 4096
EPS = 1e-5

LANES = 128
SUBLANES = 8
HALO = 16
DT_PAD = LANES
VMEM_LIMIT = 56 * 1024 * 1024

F32 = jnp.float32
BF16 = jnp.bfloat16


def _dot(a, b):
    return jnp.dot(a, b, preferred_element_type=F32)


def _sigmoid(x):
    return 1.0 / (1.0 + jnp.exp(-x))


def _any_eq(j, values):
    return functools.reduce(jnp.logical_or, [j == v for v in values])


def _stream_specs(xs, tm, tile_of):
    specs, first = [], 0
    for a in xs:
        n = a.shape[0] // tm
        specs.append(pl.BlockSpec(
            (tm, D_MODEL), lambda i, first=first, n=n: (jnp.clip(tile_of(i) - first, 0, n - 1), 0)))
        first += n
    return specs


def _stream_tile(x_refs, ends, tile):
    x = x_refs[-1][...]
    for ref, end in reversed(list(zip(x_refs[:-1], ends[:-1]))):
        x = jnp.where(tile < end, ref[...], x)
    return x


def _stream_ends(xs, tm):
    return tuple(int(v) for v in np.cumsum([a.shape[0] // tm for a in xs]))


def _inproj_kernel(*refs, ends):
    nx = len(ends)
    nw_ref, wz_ref, wx_ref, wcm_ref, dtb_ref, z_ref, xbc_ref, dt_ref, h_ref = refs[nx:]
    x = _stream_tile(refs[:nx], ends, pl.program_id(0))
    ms = jnp.mean(x * x, axis=-1, keepdims=True)
    n = (x * lax.rsqrt(ms + EPS) * nw_ref[...]).astype(BF16)
    z_ref[...] = _dot(n, wz_ref[...]).astype(BF16)
    xd = _dot(n, wx_ref[...])
    xbc_ref[...] = xd[:, :D_XBC].astype(BF16)
    v = xd[:, D_XBC:] + dtb_ref[...]
    dt_ref[...] = jnp.maximum(v, 0.0) + jnp.log(1.0 + jnp.exp(-jnp.abs(v)))
    u = _dot(n, wcm_ref[...])
    h_ref[...] = (u[:, :D_CM] * _sigmoid(u[:, D_CM:])).astype(BF16)


def _inproj(xs, nw, wz, wx, wcm, dtb, tm):
    t = sum(a.shape[0] for a in xs)
    const = lambda i: (0, 0)
    row = lambda i: (i, 0)
    return pl.pallas_call(
        functools.partial(_inproj_kernel, ends=_stream_ends(xs, tm)),
        grid=(t // tm,),
        in_specs=_stream_specs(xs, tm, lambda i: i) + [
            pl.BlockSpec((1, D_MODEL), const),
            pl.BlockSpec((D_MODEL, D_SSD), const),
            pl.BlockSpec((D_MODEL, D_XBC + DT_PAD), const),
            pl.BlockSpec((D_MODEL, 2 * D_CM), const),
            pl.BlockSpec((1, DT_PAD), const),
        ],
        out_specs=[
            pl.BlockSpec((tm, D_SSD), row),
            pl.BlockSpec((tm, D_XBC), row),
            pl.BlockSpec((tm, DT_PAD), row),
            pl.BlockSpec((tm, D_CM), row),
        ],
        out_shape=[
            jax.ShapeDtypeStruct((t, D_SSD), BF16),
            jax.ShapeDtypeStruct((t, D_XBC), BF16),
            jax.ShapeDtypeStruct((t, DT_PAD), F32),
            jax.ShapeDtypeStruct((t, D_CM), BF16),
        ],
        compiler_params=pltpu.CompilerParams(
            dimension_semantics=("arbitrary",), vmem_limit_bytes=VMEM_LIMIT),
        name="inproj",
    )(*xs, nw, wz, wx, wcm, dtb)


SSD_CONV_ROWS = 64


def _pair_words(win, pbuf, nrow):
    nword = nrow // 2
    for c in range(win.shape[1] // LANES):
        even = pltpu.bitcast(win[:, c * LANES:(c + 1) * LANES], jnp.uint32)
        nxt = pltpu.roll(even, nword - 1, axis=0)
        odd = (even >> 16) | (nxt << 16)
        pbuf[c, pl.ds(0, nword, stride=2), :] = even
        pbuf[c, pl.ds(1, nword, stride=2), :] = odd


def _scan_mask(reverse):
    li = lax.broadcasted_iota(jnp.int32, (CHUNK, CHUNK), 0)
    si = lax.broadcasted_iota(jnp.int32, (CHUNK, CHUNK), 1)
    return (si >= li) if reverse else (si <= li)


def _ssd_prepare(c, act, dt, a_row, sc, reverse):
    acs_s, acst_s, w2t_s, bgt_s, cb_s = sc
    a = dt * a_row
    tmat = jnp.where(_scan_mask(reverse), 1.0, 0.0).astype(BF16)
    a1 = a.astype(BF16)
    rem = a - a1.astype(F32)
    a2 = rem.astype(BF16)
    a3 = (rem - a2.astype(F32)).astype(BF16)
    parts = _dot(tmat, jnp.concatenate([a1, a2, a3], axis=1))
    acs = parts[:, :DT_PAD] + parts[:, DT_PAD:2 * DT_PAD] + parts[:, 2 * DT_PAD:]
    tot = acs[0:1, :] if reverse else acs[CHUNK - 1:CHUNK, :]
    w2 = dt * jnp.exp(tot - acs)
    acs_s[c * CHUNK:(c + 1) * CHUNK, :] = acs
    acst_s[c] = (acs - jnp.log(dt)).T
    w2t_s[c] = w2.T
    for g in range(N_GROUPS):
        bg = act[:, D_SSD + g * D_STATE:D_SSD + (g + 1) * D_STATE]
        cg = act[:, D_SSD + N_GROUPS * D_STATE + g * D_STATE:
                 D_SSD + N_GROUPS * D_STATE + (g + 1) * D_STATE]
        cb_s[c, g] = lax.dot_general(cg.astype(BF16), bg.astype(BF16), (((1,), (1,)), ((), ())),
                                     preferred_element_type=F32)
        bgt_s[c, g] = bg.T


def _ssd_scan_chunk(c, act_ref, sc, h_ref, y_ref, drow, reverse):
    acs_s, acst_s, w2t_s, bgt_s, cb_s = sc
    rows = pl.ds(pl.multiple_of(c * CHUNK, CHUNK), CHUNK)
    acs = acs_s[rows, :]
    acs_t = acst_s[c]
    w2_t = w2t_s[c]
    mask = _scan_mask(reverse)
    lane_lo = lax.broadcasted_iota(jnp.int32, (CHUNK, LANES), 1) < HEAD_DIM
    hoff = N_HEADS if reverse else 0
    heads_per_group = N_HEADS // N_GROUPS
    gw = heads_per_group * HEAD_DIM
    c0 = D_SSD + N_GROUPS * D_STATE
    for g in range(N_GROUPS):
        cb = cb_s[c, g]
        bg_t = bgt_s[c, g]
        cg_bf = act_ref[rows, c0 + g * D_STATE:c0 + (g + 1) * D_STATE]
        for pp in range(heads_per_group // 2):
            lanes = slice(g * gw + pp * LANES, g * gw + (pp + 1) * LANES)
            mats = []
            for hh in range(2):
                hl = hoff + g * heads_per_group + 2 * pp + hh
                colb = jnp.broadcast_to(acs[:, hl:hl + 1], (CHUNK, CHUNK))
                rowb = jnp.broadcast_to(acs_t[hl:hl + 1, :], (CHUNK, CHUNK))
                m = cb * jnp.exp(jnp.where(mask, colb - rowb, -1e30))
                bw = bg_t * jnp.broadcast_to(w2_t[hl:hl + 1, :], (CHUNK, CHUNK))
                mats.append((colb, m.astype(BF16), bw.astype(BF16)))
            (colb0, m0, bw0), (colb1, m1, bw1) = mats
            lhs = jnp.concatenate([jnp.concatenate([m0, m1], axis=1),
                                   jnp.concatenate([bw0, bw1], axis=1)], axis=0)
            xp = act_ref[rows, lanes]
            zero = jnp.zeros_like(xp)
            rhs = jnp.concatenate([jnp.where(lane_lo, xp, zero),
                                   jnp.where(lane_lo, zero, xp)], axis=0)
            out = _dot(lhs, rhs)
            ep = jnp.exp(jnp.where(lane_lo, colb0, colb1))
            yo = _dot(cg_bf, h_ref[:, lanes].astype(BF16))
            y = out[:CHUNK] + yo * ep
            if not reverse:
                y = y + xp.astype(F32) * drow[:, lanes]
            y_ref[rows, lanes] = y.astype(y_ref.dtype)
            dec = ep[0:1, :] if reverse else ep[CHUNK - 1:CHUNK, :]
            h_ref[:, lanes] = h_ref[:, lanes] * dec + out[CHUNK:]


def _scan_scratch(tb):
    nchunk = tb // CHUNK
    return [
        pltpu.VMEM((tb, DT_PAD), F32),
        pltpu.VMEM((nchunk, DT_PAD, CHUNK), F32),
        pltpu.VMEM((nchunk, DT_PAD, CHUNK), F32),
        pltpu.VMEM((nchunk, N_GROUPS, D_STATE, CHUNK), F32),
        pltpu.VMEM((nchunk, N_GROUPS, CHUNK, CHUNK), F32),
    ]


def _ssd_fwd_kernel(x_ref, xp_ref, xn_ref, dt_ref, convw_ref, convb_ref, alog_ref, drow_ref,
                    y_ref, act_ref, pbuf, cs, *rest, tb, first_blocks, last_blocks):
    sc, h_ref = rest[:-1], rest[-1]
    i = pl.program_id(0)
    is_first = _any_eq(i, first_blocks)
    is_last = _any_eq(i, last_blocks)
    ncol = D_XBC // LANES
    nchunk = tb // CHUNK
    zero = jnp.zeros((HALO, D_XBC), BF16)
    win = jnp.concatenate([jnp.where(is_first, zero, xp_ref[...]), x_ref[...],
                           jnp.where(is_last, zero, xn_ref[...])], axis=0)
    _pair_words(win, pbuf, tb + 2 * HALO)

    @pl.when(is_first)
    def _():
        h_ref[...] = jnp.zeros_like(h_ref)

    ngrp = SSD_CONV_ROWS // (2 * SUBLANES)

    def conv_body(j, carry):
        r0 = pl.multiple_of(j * SSD_CONV_ROWS, SSD_CONV_ROWS)
        for c in range(ncol):
            accs = [None] * ngrp
            for k in range(SSD_CONV):
                wk = convw_ref[k, c].astype(F32)
                for g in range(ngrp):
                    row = r0 + 2 * SUBLANES * g + HALO - SSD_CONV // 2 + k
                    words = pbuf[c, pl.ds(row, SUBLANES, stride=2), :]
                    term = pltpu.bitcast(words, BF16).astype(F32) * wk
                    accs[g] = term if accs[g] is None else accs[g] + term
            for g in range(ngrp):
                cs[c, pl.ds(r0 + 2 * SUBLANES * g, 2 * SUBLANES), :] = accs[g]
        return carry

    lax.fori_loop(0, tb // SSD_CONV_ROWS, conv_body, 0)

    a_row = -jnp.exp(alog_ref[...])
    for c in range(nchunk):
        rows = slice(c * CHUNK, (c + 1) * CHUNK)
        pre = jnp.concatenate([cs[ct, rows, :] for ct in range(ncol)], axis=1) + convb_ref[0:1, :]
        act = pre * _sigmoid(pre)
        act_ref[rows, :] = act.astype(act_ref.dtype)
        _ssd_prepare(c, act, dt_ref[rows, :], a_row, sc, False)

    drow = drow_ref[...]

    def body(c, carry):
        _ssd_scan_chunk(c, act_ref, sc, h_ref, y_ref, drow, False)
        return carry

    lax.fori_loop(0, nchunk, body, 0, unroll=True)


def _ssd_bwd_kernel(act_ref, dt_ref, alog_ref, y_ref, *rest, tb, last_blocks):
    sc, h_ref = rest[:-1], rest[-1]
    nb = pl.num_programs(0)
    j = nb - 1 - pl.program_id(0)

    @pl.when(_any_eq(j, last_blocks))
    def _():
        h_ref[...] = jnp.zeros_like(h_ref)

    a_row = -jnp.exp(alog_ref[...])
    nchunk = tb // CHUNK
    for c in range(nchunk):
        rows = slice(c * CHUNK, (c + 1) * CHUNK)
        _ssd_prepare(c, act_ref[rows, :].astype(F32), dt_ref[rows, :], a_row, sc, True)

    def body(ci, carry):
        _ssd_scan_chunk(nchunk - 1 - ci, act_ref, sc, h_ref, y_ref, None, True)
        return carry

    lax.fori_loop(0, nchunk, body, 0, unroll=True)


def _ssd_fwd(xbc, dt, convw, convb, alog, drow, tb, first_blocks, last_blocks):
    t = xbc.shape[0]
    hb = tb // HALO
    nh = t // HALO
    const = lambda i: (0, 0)
    row = lambda i: (i, 0)
    prev = lambda i: (jnp.maximum(i * hb - 1, 0), 0)
    nxt = lambda i: (jnp.minimum((i + 1) * hb, nh - 1), 0)
    kern = functools.partial(_ssd_fwd_kernel, tb=tb, first_blocks=first_blocks,
                             last_blocks=last_blocks)
    return pl.pallas_call(
        kern,
        grid=(t // tb,),
        in_specs=[
            pl.BlockSpec((tb, D_XBC), row),
            pl.BlockSpec((HALO, D_XBC), prev),
            pl.BlockSpec((HALO, D_XBC), nxt),
            pl.BlockSpec((tb, DT_PAD), row),
            pl.BlockSpec((SSD_CONV, D_XBC // LANES, 2 * SUBLANES, LANES), lambda i: (0, 0, 0, 0)),
            pl.BlockSpec((SUBLANES, D_XBC), const),
            pl.BlockSpec((1, DT_PAD), const),
            pl.BlockSpec((1, D_SSD), const),
        ],
        out_specs=[pl.BlockSpec((tb, D_SSD), row), pl.BlockSpec((tb, D_XBC), row)],
        out_shape=[jax.ShapeDtypeStruct((t, D_SSD), BF16), jax.ShapeDtypeStruct((t, D_XBC), BF16)],
        scratch_shapes=[
            pltpu.VMEM((D_XBC // LANES, tb + 2 * HALO, LANES), jnp.uint32),
            pltpu.VMEM((D_XBC // LANES, tb, LANES), F32),
        ] + _scan_scratch(tb) + [pltpu.VMEM((D_STATE, D_SSD), F32)],
        compiler_params=pltpu.CompilerParams(
            dimension_semantics=("arbitrary",), vmem_limit_bytes=VMEM_LIMIT),
        name="ssd_fwd",
    )(xbc, xbc, xbc, dt, convw, convb, alog, drow)


def _ssd_bwd(act, dt, alog, tb, last_blocks):
    t = act.shape[0]
    nb = t // tb
    const = lambda i: (0, 0)
    rev = lambda i: (nb - 1 - i, 0)
    kern = functools.partial(_ssd_bwd_kernel, tb=tb, last_blocks=last_blocks)
    return pl.pallas_call(
        kern,
        grid=(nb,),
        in_specs=[
            pl.BlockSpec((tb, D_XBC), rev),
            pl.BlockSpec((tb, DT_PAD), rev),
            pl.BlockSpec((1, DT_PAD), const),
        ],
        out_specs=pl.BlockSpec((tb, D_SSD), rev),
        out_shape=jax.ShapeDtypeStruct((t, D_SSD), BF16),
        scratch_shapes=_scan_scratch(tb) + [pltpu.VMEM((D_STATE, D_SSD), F32)],
        compiler_params=pltpu.CompilerParams(
            dimension_semantics=("arbitrary",), vmem_limit_bytes=VMEM_LIMIT),
        name="ssd_bwd",
    )(act, dt, alog)


CONV_ROWS = 64


def _conv31_rows(pbuf, dww_ref, cm_ref, r0):
    ngrp = CONV_ROWS // (2 * SUBLANES)
    for c in range(D_CM // LANES):
        accs = [None] * ngrp
        for k in range(CM_CONV):
            wk = dww_ref[k, c].astype(F32)
            for g in range(ngrp):
                words = pbuf[c, pl.ds(r0 + 2 * SUBLANES * g + k + 1, SUBLANES, stride=2), :]
                term = pltpu.bitcast(words, BF16).astype(F32) * wk
                accs[g] = term if accs[g] is None else accs[g] + term
        for g in range(ngrp):
            cm_ref[c, pl.ds(r0 + 2 * SUBLANES * g, 2 * SUBLANES), :] = accs[g]


def _mix_kernel(*refs, tm, nb, ends, out_ends, first_blocks, last_blocks, final):
    nx, no = len(ends), len(out_ends)
    (yf_ref, yb_ref, z_ref, h_ref, hp_ref, hn_ref, gnw_ref, dww_ref, dwb_ref, lnw_ref, lnb_ref,
     wout_ref, n2w_ref, w1_ref, w2_ref, fnw_ref) = refs[nx:nx + 16]
    o_refs = refs[nx + 16:nx + 16 + no]
    pbuf, cm_ref, n2_ref, acc_ref, hm_ref = refs[nx + 16 + no:]
    i = pl.program_id(0)
    a = jnp.minimum(i, nb - 1)
    is_first = _any_eq(a, first_blocks)
    is_last = _any_eq(a, last_blocks)

    @pl.when(i == 0)
    def _():
        cm_ref[...] = jnp.zeros_like(cm_ref)

    zf = z_ref[...].astype(F32)
    y = (yf_ref[...].astype(F32) + yb_ref[...].astype(F32)) * (zf * _sigmoid(zf))
    gw = D_SSD // N_GROUPS
    parts = []
    for g in range(N_GROUPS):
        yg = y[:, g * gw:(g + 1) * gw]
        ms = jnp.mean(yg * yg, axis=-1, keepdims=True)
        parts.append(yg * lax.rsqrt(ms + EPS))
    y_ssd = (jnp.concatenate(parts, axis=1) * gnw_ref[...]).astype(BF16)

    ncol = D_CM // LANES
    x = _stream_tile(refs[:nx], ends, jnp.maximum(i - 1, 0))
    acc_ref[...] = x + _dot(y_ssd, wout_ref[0:D_SSD, :])
    cm = jnp.concatenate([cm_ref[c] for c in range(ncol)], axis=1) + dwb_ref[0:1, :]
    mu = jnp.mean(cm, axis=-1, keepdims=True)
    xc = cm - mu
    var = jnp.mean(xc * xc, axis=-1, keepdims=True)
    ln = xc * lax.rsqrt(var + EPS) * lnw_ref[0:1, :] + lnb_ref[0:1, :]
    y_cm = (ln * _sigmoid(ln)).astype(BF16)
    acc_ref[...] += _dot(y_cm, wout_ref[D_SSD:, :])
    x1 = acc_ref[...]

    ms2 = jnp.mean(x1 * x1, axis=-1, keepdims=True)
    n2_ref[...] = (x1 * lax.rsqrt(ms2 + EPS) * n2w_ref[...]).astype(BF16)

    zero = jnp.zeros((HALO, D_CM), BF16)
    win = jnp.concatenate([jnp.where(is_first, zero, hp_ref[...]), h_ref[...],
                           jnp.where(is_last, zero, hn_ref[...])], axis=0)
    _pair_words(win, pbuf, tm + 2 * HALO)

    fs = w1_ref.shape[1]
    nslice = tm // CONV_ROWS

    def up(j):
        w1j = w1_ref[pl.ds(pl.multiple_of(j * D_MODEL, D_MODEL), D_MODEL), :]
        hmid = jnp.maximum(_dot(n2_ref[...], w1j), 0.0)
        return (hmid * hmid).astype(BF16)

    def down(j, slot):
        w2j = w2_ref[pl.ds(pl.multiple_of(j * fs, fs), fs), :]
        acc_ref[...] += _dot(hm_ref[slot], w2j)

    def conv(j):
        _conv31_rows(pbuf, dww_ref, cm_ref, pl.multiple_of(j * CONV_ROWS, CONV_ROWS))

    hm_ref[0] = up(0)

    def body(jj, carry):
        j = 2 * jj
        hm_ref[1] = up(j + 1)
        down(j, 0)
        conv(j)
        hm_ref[0] = up(j + 2)
        down(j + 1, 1)
        conv(j + 1)
        return carry

    lax.fori_loop(0, nslice // 2 - 1, body, 0)
    hm_ref[1] = up(nslice - 1)
    down(nslice - 2, 0)
    conv(nslice - 2)
    down(nslice - 1, 1)
    conv(nslice - 1)
    acc = acc_ref[...]
    if final:
        msf = jnp.mean(acc * acc, axis=-1, keepdims=True)
        acc = acc * lax.rsqrt(msf + EPS) * fnw_ref[...]
    if no == 1:
        o_refs[0][...] = acc
    else:
        b = jnp.maximum(i - 1, 0)
        first = 0
        for o_ref, end in zip(o_refs, out_ends):
            @pl.when(jnp.logical_and(b >= first, b < end))
            def _(o_ref=o_ref):
                o_ref[...] = acc
            first = end


def _mix(xs, yf, yb, z, h, gnw, dww, dwb, lnw, lnb, wout, n2w, w1, w2, fnw,
         tm, first_blocks, last_blocks, final, out_rows):
    t = sum(a.shape[0] for a in xs)
    nb = t // tm
    outs = [jax.ShapeDtypeStruct((r, D_MODEL), F32) for r in out_rows]
    hb = tm // HALO
    nh = t // HALO
    const = lambda i: (0, 0)
    done = lambda i: (jnp.maximum(i - 1, 0), 0)
    nxt_tile = lambda i: (jnp.minimum(i, nb - 1), 0)
    prev = lambda i: (jnp.maximum(jnp.minimum(i, nb - 1) * hb - 1, 0), 0)
    nxt = lambda i: (jnp.minimum((jnp.minimum(i, nb - 1) + 1) * hb, nh - 1), 0)
    once = pl.Buffered(1)
    nslice = tm // CONV_ROWS
    fs = D_FF // nslice
    w1 = w1.reshape(D_MODEL, nslice, fs).transpose(1, 0, 2).reshape(nslice * D_MODEL, fs)
    kern = functools.partial(_mix_kernel, tm=tm, nb=nb, ends=_stream_ends(xs, tm),
                             out_ends=_stream_ends(outs, tm), first_blocks=first_blocks,
                             last_blocks=last_blocks, final=final)
    return pl.pallas_call(
        kern,
        grid=(nb + 1,),
        in_specs=_stream_specs(xs, tm, lambda i: jnp.maximum(i - 1, 0)) + [
            pl.BlockSpec((tm, D_SSD), done),
            pl.BlockSpec((tm, D_SSD), done),
            pl.BlockSpec((tm, D_SSD), done),
            pl.BlockSpec((tm, D_CM), nxt_tile),
            pl.BlockSpec((HALO, D_CM), prev),
            pl.BlockSpec((HALO, D_CM), nxt),
            pl.BlockSpec((1, D_SSD), const),
            pl.BlockSpec((CM_CONV, D_CM // LANES, 2 * SUBLANES, LANES), lambda i: (0, 0, 0, 0)),
            pl.BlockSpec((SUBLANES, D_CM), const),
            pl.BlockSpec((SUBLANES, D_CM), const),
            pl.BlockSpec((SUBLANES, D_CM), const),
            pl.BlockSpec((D_SSD + D_CM, D_MODEL), const, pipeline_mode=once),
            pl.BlockSpec((1, D_MODEL), const),
            pl.BlockSpec((nslice * D_MODEL, fs), const, pipeline_mode=once),
            pl.BlockSpec((D_FF, D_MODEL), const, pipeline_mode=once),
            pl.BlockSpec((1, D_MODEL), const),
        ],
        out_specs=_stream_specs(outs, tm, lambda i: jnp.maximum(i - 1, 0)),
        out_shape=outs,
        scratch_shapes=[
            pltpu.VMEM((D_CM // LANES, tm + 2 * HALO, LANES), jnp.uint32),
            pltpu.VMEM((D_CM // LANES, tm, LANES), F32),
            pltpu.VMEM((tm, D_MODEL), BF16),
            pltpu.VMEM((tm, D_MODEL), F32),
            pltpu.VMEM((2, tm, fs), BF16),
        ],
        compiler_params=pltpu.CompilerParams(
            dimension_semantics=("arbitrary",), vmem_limit_bytes=VMEM_LIMIT),
        name="mix_mlp",
    )(*xs, yf, yb, z, h, h, h, gnw, dww, dwb, lnw, lnb, wout, n2w, w1, w2, fnw)


def _block_size(seq_lens, want):
    b = want
    while any(s % b for s in seq_lens):
        b //= 2
    assert b >= CHUNK, "sequence lengths must be multiples of the SSD chunk"
    return b


def kernel(x_prompt, x_sample, norm1_w, w_in, ssd_conv_w, ssd_conv_b, dt_bias, a_log, d_skip,
           ssd_norm_w, cm_dw_w, cm_dw_b, cm_ln_w, cm_ln_b, w_out, norm2_w, w_mlp_in, w_mlp_out,
           final_norm_w):
    depth = w_in.shape[0]
    seq_lens = [x_prompt.shape[1]] * x_prompt.shape[0] + [x_sample.shape[1]] * x_sample.shape[0]
    n_prompt = x_prompt.shape[0] * x_prompt.shape[1]
    xs = (x_prompt.reshape(-1, D_MODEL), x_sample.reshape(-1, D_MODEL))

    tb = _block_size(seq_lens, 512)
    starts = np.concatenate([[0], np.cumsum(seq_lens)[:-1]])
    ends = np.cumsum(seq_lens)
    first_blocks = tuple(int(s) // tb for s in starts)
    last_blocks = tuple(int(e) // tb - 1 for e in ends)

    s1 = D_SSD
    s2 = s1 + D_XBC
    s3 = s2 + 2 * N_HEADS
    row = lambda v: v.reshape(1, -1).astype(F32)
    for l in range(depth):
        wl = w_in[l]
        wz = wl[:, :s1].astype(BF16)
        wx = jnp.concatenate([wl[:, s1:s3], jnp.zeros((D_MODEL, DT_PAD - 2 * N_HEADS), F32)],
                             axis=1).astype(BF16)
        wcm = wl[:, s3:].astype(BF16)
        dtb = jnp.pad(dt_bias[l].reshape(1, -1), ((0, 0), (0, DT_PAD - 2 * N_HEADS)))
        alog = jnp.pad(a_log[l].reshape(1, -1), ((0, 0), (0, DT_PAD - 2 * N_HEADS)))
        z, xbc, dt, h = _inproj(xs, row(norm1_w[l]), wz, wx, wcm, dtb, tb)

        rep = lambda v: jnp.broadcast_to(v.reshape(1, -1).astype(F32), (SUBLANES, v.size))
        convw = jnp.broadcast_to(
            ssd_conv_w[l].astype(BF16).reshape(SSD_CONV, D_XBC // LANES, 1, LANES),
            (SSD_CONV, D_XBC // LANES, 2 * SUBLANES, LANES))
        drow = jnp.repeat(d_skip[l], HEAD_DIM).reshape(1, -1)
        yf, act = _ssd_fwd(xbc, dt, convw, rep(ssd_conv_b[l]), alog, drow, tb,
                           first_blocks, last_blocks)
        yb = _ssd_bwd(act, dt, alog, tb, last_blocks)

        dww = jnp.broadcast_to(cm_dw_w[l].astype(BF16).reshape(CM_CONV, D_CM // LANES, 1, LANES),
                               (CM_CONV, D_CM // LANES, 2 * SUBLANES, LANES))
        last = l == depth - 1
        n_sample = x_sample.shape[0] * x_sample.shape[1]
        out_rows = (n_prompt, n_sample) if last else (n_prompt + n_sample,)
        x = _mix(xs, yf, yb, z, h, row(ssd_norm_w[l]), dww, rep(cm_dw_b[l]), rep(cm_ln_w[l]),
                 rep(cm_ln_b[l]), w_out[l].astype(BF16), row(norm2_w[l]),
                 w_mlp_in[l].astype(BF16), w_mlp_out[l].astype(BF16), row(final_norm_w),
                 tb, first_blocks, last_blocks, last, out_rows)
        xs = tuple(x)

    y_prompt, y_sample = xs
    return y_prompt.reshape(x_prompt.shape), y_sample.reshape(x_sample.shape)
```

```python
import functools

import jax
import jax.numpy as jnp
import numpy as np
from jax import lax
from jax.experimental import pallas as pl
from jax.experimental.pallas import tpu as pltpu

D_MODEL = 1024
D_SSD = 1024
HEAD_DIM = 64
N_HEADS = 16
N_GROUPS = 2
D_STATE = 128
CHUNK = 128
D_XBC = D_SSD + 2 * N_GROUPS * D_STATE
D_CM = 1024
SSD_CONV = 5
CM_CONV = 31
D_FF = 4096
EPS = 1e-5

LANES = 128
SUBLANES = 8
HALO = 16
DT_PAD = LANES
VMEM_LIMIT = 56 * 1024 * 1024

F32 = jnp.float32
BF16 = jnp.bfloat16


def _dot(a, b):
    return jnp.dot(a, b, preferred_element_type=F32)


def _sigmoid(x):
    return 1.0 / (1.0 + jnp.exp(-x))


def _any_eq(j, values):
    return functools.reduce(jnp.logical_or, [j == v for v in values])


def _stream_specs(xs, tm, tile_of):
    specs, first = [], 0
    for a in xs:
        n = a.shape[0] // tm
        specs.append(pl.BlockSpec(
            (tm, D_MODEL), lambda i, first=first, n=n: (jnp.clip(tile_of(i) - first, 0, n - 1), 0)))
        first += n
    return specs


def _stream_tile(x_refs, ends, tile):
    x = x_refs[-1][...]
    for ref, end in reversed(list(zip(x_refs[:-1], ends[:-1]))):
        x = jnp.where(tile < end, ref[...], x)
    return x


def _stream_ends(xs, tm):
    return tuple(int(v) for v in np.cumsum([a.shape[0] // tm for a in xs]))


def _inproj_kernel(*refs, ends):
    nx = len(ends)
    nw_ref, wz_ref, wx_ref, wcm_ref, dtb_ref, z_ref, xbc_ref, dt_ref, h_ref = refs[nx:]
    x = _stream_tile(refs[:nx], ends, pl.program_id(0))
    ms = jnp.mean(x * x, axis=-1, keepdims=True)
    n = (x * lax.rsqrt(ms + EPS) * nw_ref[...]).astype(BF16)
    z_ref[...] = _dot(n, wz_ref[...]).astype(BF16)
    xd = _dot(n, wx_ref[...])
    xbc_ref[...] = xd[:, :D_XBC].astype(BF16)
    v = xd[:, D_XBC:] + dtb_ref[...]
    dt_ref[...] = jnp.maximum(v, 0.0) + jnp.log(1.0 + jnp.exp(-jnp.abs(v)))
    u = _dot(n, wcm_ref[...])
    h_ref[...] = (u[:, :D_CM] * _sigmoid(u[:, D_CM:])).astype(BF16)


def _inproj(xs, nw, wz, wx, wcm, dtb, tm):
    t = sum(a.shape[0] for a in xs)
    const = lambda i: (0, 0)
    row = lambda i: (i, 0)
    once = pl.Buffered(1)
    return pl.pallas_call(
        functools.partial(_inproj_kernel, ends=_stream_ends(xs, tm)),
        grid=(t // tm,),
        in_specs=_stream_specs(xs, tm, lambda i: i) + [
            pl.BlockSpec((1, D_MODEL), const),
            pl.BlockSpec((D_MODEL, D_SSD), const, pipeline_mode=once),
            pl.BlockSpec((D_MODEL, D_XBC + DT_PAD), const, pipeline_mode=once),
            pl.BlockSpec((D_MODEL, 2 * D_CM), const, pipeline_mode=once),
            pl.BlockSpec((1, DT_PAD), const),
        ],
        out_specs=[
            pl.BlockSpec((tm, D_SSD), row),
            pl.BlockSpec((tm, D_XBC), row),
            pl.BlockSpec((tm, DT_PAD), row),
            pl.BlockSpec((tm, D_CM), row),
        ],
        out_shape=[
            jax.ShapeDtypeStruct((t, D_SSD), BF16),
            jax.ShapeDtypeStruct((t, D_XBC), BF16),
            jax.ShapeDtypeStruct((t, DT_PAD), F32),
            jax.ShapeDtypeStruct((t, D_CM), BF16),
        ],
        compiler_params=pltpu.CompilerParams(
            dimension_semantics=("arbitrary",), vmem_limit_bytes=VMEM_LIMIT),
        name="inproj",
    )(*xs, nw, wz, wx, wcm, dtb)


SSD_CONV_ROWS = 64


def _pair_words(win, pbuf, nrow):
    nword = nrow // 2
    for c in range(win.shape[1] // LANES):
        even = pltpu.bitcast(win[:, c * LANES:(c + 1) * LANES], jnp.uint32)
        nxt = pltpu.roll(even, nword - 1, axis=0)
        odd = (even >> 16) | (nxt << 16)
        pbuf[c, pl.ds(0, nword, stride=2), :] = even
        pbuf[c, pl.ds(1, nword, stride=2), :] = odd


def _scan_mask(reverse):
    li = lax.broadcasted_iota(jnp.int32, (CHUNK, CHUNK), 0)
    si = lax.broadcasted_iota(jnp.int32, (CHUNK, CHUNK), 1)
    return (si >= li) if reverse else (si <= li)


def _ssd_prepare(c, act, dt, a_row, sc, reverse):
    acs_s, acst_s, w2t_s, bgt_s, cb_s = sc
    a = dt * a_row
    tmat = jnp.where(_scan_mask(reverse), 1.0, 0.0).astype(BF16)
    a1 = a.astype(BF16)
    rem = a - a1.astype(F32)
    a2 = rem.astype(BF16)
    a3 = (rem - a2.astype(F32)).astype(BF16)
    parts = _dot(tmat, jnp.concatenate([a1, a2, a3], axis=1))
    acs = parts[:, :DT_PAD] + parts[:, DT_PAD:2 * DT_PAD] + parts[:, 2 * DT_PAD:]
    tot = acs[0:1, :] if reverse else acs[CHUNK - 1:CHUNK, :]
    w2 = dt * jnp.exp(tot - acs)
    acs_s[c * CHUNK:(c + 1) * CHUNK, :] = acs
    acst_s[c] = (acs - jnp.log(dt)).T
    w2t_s[c] = w2.T
    for g in range(N_GROUPS):
        bg = act[:, D_SSD + g * D_STATE:D_SSD + (g + 1) * D_STATE]
        cg = act[:, D_SSD + N_GROUPS * D_STATE + g * D_STATE:
                 D_SSD + N_GROUPS * D_STATE + (g + 1) * D_STATE]
        cb_s[c, g] = lax.dot_general(cg.astype(BF16), bg.astype(BF16), (((1,), (1,)), ((), ())),
                                     preferred_element_type=F32)
        bgt_s[c, g] = bg.T


def _ssd_scan_chunk(c, act_ref, sc, h_ref, y_ref, drow, reverse):
    acs_s, acst_s, w2t_s, bgt_s, cb_s = sc
    rows = pl.ds(pl.multiple_of(c * CHUNK, CHUNK), CHUNK)
    acs = acs_s[rows, :]
    acs_t = acst_s[c]
    w2_t = w2t_s[c]
    mask = _scan_mask(reverse)
    lane_lo = lax.broadcasted_iota(jnp.int32, (CHUNK, LANES), 1) < HEAD_DIM
    hoff = N_HEADS if reverse else 0
    heads_per_group = N_HEADS // N_GROUPS
    gw = heads_per_group * HEAD_DIM
    c0 = D_SSD + N_GROUPS * D_STATE
    for g in range(N_GROUPS):
        cb = cb_s[c, g]
        bg_t = bgt_s[c, g]
        cg_bf = act_ref[rows, c0 + g * D_STATE:c0 + (g + 1) * D_STATE]
        for pp in range(heads_per_group // 2):
            lanes = slice(g * gw + pp * LANES, g * gw + (pp + 1) * LANES)
            mats = []
            for hh in range(2):
                hl = hoff + g * heads_per_group + 2 * pp + hh
                colb = jnp.broadcast_to(acs[:, hl:hl + 1], (CHUNK, CHUNK))
                rowb = jnp.broadcast_to(acs_t[hl:hl + 1, :], (CHUNK, CHUNK))
                m = cb * jnp.exp(jnp.where(mask, colb - rowb, -1e30))
                bw = bg_t * jnp.broadcast_to(w2_t[hl:hl + 1, :], (CHUNK, CHUNK))
                mats.append((colb, m.astype(BF16), bw.astype(BF16)))
            (colb0, m0, bw0), (colb1, m1, bw1) = mats
            lhs = jnp.concatenate([jnp.concatenate([m0, m1], axis=1),
                                   jnp.concatenate([bw0, bw1], axis=1)], axis=0)
            xp = act_ref[rows, lanes]
            zero = jnp.zeros_like(xp)
            rhs = jnp.concatenate([jnp.where(lane_lo, xp, zero),
                                   jnp.where(lane_lo, zero, xp)], axis=0)
            out = _dot(lhs, rhs)
            ep = jnp.exp(jnp.where(lane_lo, colb0, colb1))
            yo = _dot(cg_bf, h_ref[:, lanes].astype(BF16))
            y = out[:CHUNK] + yo * ep
            if not reverse:
                y = y + xp.astype(F32) * drow[:, lanes]
            y_ref[rows, lanes] = y.astype(y_ref.dtype)
            dec = ep[0:1, :] if reverse else ep[CHUNK - 1:CHUNK, :]
            h_ref[:, lanes] = h_ref[:, lanes] * dec + out[CHUNK:]


def _scan_scratch(tb):
    nchunk = tb // CHUNK
    return [
        pltpu.VMEM((tb, DT_PAD), F32),
        pltpu.VMEM((nchunk, DT_PAD, CHUNK), F32),
        pltpu.VMEM((nchunk, DT_PAD, CHUNK), F32),
        pltpu.VMEM((nchunk, N_GROUPS, D_STATE, CHUNK), F32),
        pltpu.VMEM((nchunk, N_GROUPS, CHUNK, CHUNK), F32),
    ]


def _ssd_fwd_kernel(x_ref, xp_ref, xn_ref, dt_ref, convw_ref, convb_ref, alog_ref, drow_ref,
                    y_ref, act_ref, pbuf, cs, *rest, tb, first_blocks, last_blocks):
    sc, h_ref = rest[:-1], rest[-1]
    i = pl.program_id(0)
    is_first = _any_eq(i, first_blocks)
    is_last = _any_eq(i, last_blocks)
    ncol = D_XBC // LANES
    nchunk = tb // CHUNK
    zero = jnp.zeros((HALO, D_XBC), BF16)
    win = jnp.concatenate([jnp.where(is_first, zero, xp_ref[...]), x_ref[...],
                           jnp.where(is_last, zero, xn_ref[...])], axis=0)
    _pair_words(win, pbuf, tb + 2 * HALO)

    @pl.when(is_first)
    def _():
        h_ref[...] = jnp.zeros_like(h_ref)

    ngrp = SSD_CONV_ROWS // (2 * SUBLANES)

    def conv_body(j, carry):
        r0 = pl.multiple_of(j * SSD_CONV_ROWS, SSD_CONV_ROWS)
        for c in range(ncol):
            accs = [None] * ngrp
            for k in range(SSD_CONV):
                wk = convw_ref[k, c].astype(F32)
                for g in range(ngrp):
                    row = r0 + 2 * SUBLANES * g + HALO - SSD_CONV // 2 + k
                    words = pbuf[c, pl.ds(row, SUBLANES, stride=2), :]
                    term = pltpu.bitcast(words, BF16).astype(F32) * wk
                    accs[g] = term if accs[g] is None else accs[g] + term
            for g in range(ngrp):
                cs[c, pl.ds(r0 + 2 * SUBLANES * g, 2 * SUBLANES), :] = accs[g]
        return carry

    lax.fori_loop(0, tb // SSD_CONV_ROWS, conv_body, 0)

    a_row = -jnp.exp(alog_ref[...])
    for c in range(nchunk):
        rows = slice(c * CHUNK, (c + 1) * CHUNK)
        pre = jnp.concatenate([cs[ct, rows, :] for ct in range(ncol)], axis=1) + convb_ref[0:1, :]
        act = pre * _sigmoid(pre)
        act_ref[rows, :] = act.astype(act_ref.dtype)
        _ssd_prepare(c, act, dt_ref[rows, :], a_row, sc, False)

    drow = drow_ref[...]

    def body(c, carry):
        _ssd_scan_chunk(c, act_ref, sc, h_ref, y_ref, drow, False)
        return carry

    lax.fori_loop(0, nchunk, body, 0, unroll=True)


def _ssd_bwd_kernel(act_ref, dt_ref, alog_ref, y_ref, *rest, tb, last_blocks):
    sc, h_ref = rest[:-1], rest[-1]
    nb = pl.num_programs(0)
    j = nb - 1 - pl.program_id(0)

    @pl.when(_any_eq(j, last_blocks))
    def _():
        h_ref[...] = jnp.zeros_like(h_ref)

    a_row = -jnp.exp(alog_ref[...])
    nchunk = tb // CHUNK
    for c in range(nchunk):
        rows = slice(c * CHUNK, (c + 1) * CHUNK)
        _ssd_prepare(c, act_ref[rows, :].astype(F32), dt_ref[rows, :], a_row, sc, True)

    def body(ci, carry):
        _ssd_scan_chunk(nchunk - 1 - ci, act_ref, sc, h_ref, y_ref, None, True)
        return carry

    lax.fori_loop(0, nchunk, body, 0, unroll=True)


def _ssd_fwd(xbc, dt, convw, convb, alog, drow, tb, first_blocks, last_blocks):
    t = xbc.shape[0]
    hb = tb // HALO
    nh = t // HALO
    const = lambda i: (0, 0)
    row = lambda i: (i, 0)
    prev = lambda i: (jnp.maximum(i * hb - 1, 0), 0)
    nxt = lambda i: (jnp.minimum((i + 1) * hb, nh - 1), 0)
    kern = functools.partial(_ssd_fwd_kernel, tb=tb, first_blocks=first_blocks,
                             last_blocks=last_blocks)
    return pl.pallas_call(
        kern,
        grid=(t // tb,),
        in_specs=[
            pl.BlockSpec((tb, D_XBC), row),
            pl.BlockSpec((HALO, D_XBC), prev),
            pl.BlockSpec((HALO, D_XBC), nxt),
            pl.BlockSpec((tb, DT_PAD), row),
            pl.BlockSpec((SSD_CONV, D_XBC // LANES, 2 * SUBLANES, LANES), lambda i: (0, 0, 0, 0)),
            pl.BlockSpec((SUBLANES, D_XBC), const),
            pl.BlockSpec((1, DT_PAD), const),
            pl.BlockSpec((1, D_SSD), const),
        ],
        out_specs=[pl.BlockSpec((tb, D_SSD), row), pl.BlockSpec((tb, D_XBC), row)],
        out_shape=[jax.ShapeDtypeStruct((t, D_SSD), BF16), jax.ShapeDtypeStruct((t, D_XBC), BF16)],
        scratch_shapes=[
            pltpu.VMEM((D_XBC // LANES, tb + 2 * HALO, LANES), jnp.uint32),
            pltpu.VMEM((D_XBC // LANES, tb, LANES), F32),
        ] + _scan_scratch(tb) + [pltpu.VMEM((D_STATE, D_SSD), F32)],
        compiler_params=pltpu.CompilerParams(
            dimension_semantics=("arbitrary",), vmem_limit_bytes=VMEM_LIMIT),
        name="ssd_fwd",
    )(xbc, xbc, xbc, dt, convw, convb, alog, drow)


def _ssd_bwd(act, dt, alog, tb, last_blocks):
    t = act.shape[0]
    nb = t // tb
    const = lambda i: (0, 0)
    rev = lambda i: (nb - 1 - i, 0)
    kern = functools.partial(_ssd_bwd_kernel, tb=tb, last_blocks=last_blocks)
    return pl.pallas_call(
        kern,
        grid=(nb,),
        in_specs=[
            pl.BlockSpec((tb, D_XBC), rev),
            pl.BlockSpec((tb, DT_PAD), rev),
            pl.BlockSpec((1, DT_PAD), const),
        ],
        out_specs=pl.BlockSpec((tb, D_SSD), rev),
        out_shape=jax.ShapeDtypeStruct((t, D_SSD), BF16),
        scratch_shapes=_scan_scratch(tb) + [pltpu.VMEM((D_STATE, D_SSD), F32)],
        compiler_params=pltpu.CompilerParams(
            dimension_semantics=("arbitrary",), vmem_limit_bytes=VMEM_LIMIT),
        name="ssd_bwd",
    )(act, dt, alog)


CONV_ROWS = 64


def _conv31_rows(pbuf, dww_ref, cm_ref, r0):
    ngrp = CONV_ROWS // (2 * SUBLANES)
    for c in range(D_CM // LANES):
        accs = [None] * ngrp
        for k in range(CM_CONV):
            wk = dww_ref[k, c].astype(F32)
            for g in range(ngrp):
                words = pbuf[c, pl.ds(r0 + 2 * SUBLANES * g + k + 1, SUBLANES, stride=2), :]
                term = pltpu.bitcast(words, BF16).astype(F32) * wk
                accs[g] = term if accs[g] is None else accs[g] + term
        for g in range(ngrp):
            cm_ref[c, pl.ds(r0 + 2 * SUBLANES * g, 2 * SUBLANES), :] = accs[g]


def _mix_kernel(*refs, tm, nb, ends, out_ends, first_blocks, last_blocks, final):
    nx, no = len(ends), len(out_ends)
    (yf_ref, yb_ref, z_ref, h_ref, hp_ref, hn_ref, gnw_ref, dww_ref, dwb_ref, lnw_ref, lnb_ref,
     wout_ref, n2w_ref, w1_ref, w2_ref, fnw_ref) = refs[nx:nx + 16]
    o_refs = refs[nx + 16:nx + 16 + no]
    pbuf, cm_ref, n2_ref, acc_ref, hm_ref = refs[nx + 16 + no:]
    i = pl.program_id(0)
    a = jnp.minimum(i, nb - 1)
    is_first = _any_eq(a, first_blocks)
    is_last = _any_eq(a, last_blocks)

    @pl.when(i == 0)
    def _():
        cm_ref[...] = jnp.zeros_like(cm_ref)

    zf = z_ref[...].astype(F32)
    y = (yf_ref[...].astype(F32) + yb_ref[...].astype(F32)) * (zf * _sigmoid(zf))
    gw = D_SSD // N_GROUPS
    parts = []
    for g in range(N_GROUPS):
        yg = y[:, g * gw:(g + 1) * gw]
        ms = jnp.mean(yg * yg, axis=-1, keepdims=True)
        parts.append(yg * lax.rsqrt(ms + EPS))
    y_ssd = (jnp.concatenate(parts, axis=1) * gnw_ref[...]).astype(BF16)

    ncol = D_CM // LANES
    x = _stream_tile(refs[:nx], ends, jnp.maximum(i - 1, 0))
    acc_ref[...] = x + _dot(y_ssd, wout_ref[0:D_SSD, :])
    cm = jnp.concatenate([cm_ref[c] for c in range(ncol)], axis=1) + dwb_ref[0:1, :]
    mu = jnp.mean(cm, axis=-1, keepdims=True)
    xc = cm - mu
    var = jnp.mean(xc * xc, axis=-1, keepdims=True)
    ln = xc * lax.rsqrt(var + EPS) * lnw_ref[0:1, :] + lnb_ref[0:1, :]
    y_cm = (ln * _sigmoid(ln)).astype(BF16)
    acc_ref[...] += _dot(y_cm, wout_ref[D_SSD:, :])
    x1 = acc_ref[...]

    ms2 = jnp.mean(x1 * x1, axis=-1, keepdims=True)
    n2_ref[...] = (x1 * lax.rsqrt(ms2 + EPS) * n2w_ref[...]).astype(BF16)

    zero = jnp.zeros((HALO, D_CM), BF16)
    win = jnp.concatenate([jnp.where(is_first, zero, hp_ref[...]), h_ref[...],
                           jnp.where(is_last, zero, hn_ref[...])], axis=0)
    _pair_words(win, pbuf, tm + 2 * HALO)

    fs = w1_ref.shape[1]
    nslice = tm // CONV_ROWS

    def up(j):
        w1j = w1_ref[pl.ds(pl.multiple_of(j * D_MODEL, D_MODEL), D_MODEL), :]
        hmid = jnp.maximum(_dot(n2_ref[...], w1j), 0.0)
        return (hmid * hmid).astype(BF16)

    def down(j, slot):
        w2j = w2_ref[pl.ds(pl.multiple_of(j * fs, fs), fs), :]
        acc_ref[...] += _dot(hm_ref[slot], w2j)

    def conv(j):
        _conv31_rows(pbuf, dww_ref, cm_ref, pl.multiple_of(j * CONV_ROWS, CONV_ROWS))

    hm_ref[0] = up(0)

    def body(jj, carry):
        j = 2 * jj
        hm_ref[1] = up(j + 1)
        down(j, 0)
        conv(j)
        hm_ref[0] = up(j + 2)
        down(j + 1, 1)
        conv(j + 1)
        return carry

    lax.fori_loop(0, nslice // 2 - 1, body, 0)
    hm_ref[1] = up(nslice - 1)
    down(nslice - 2, 0)
    conv(nslice - 2)
    down(nslice - 1, 1)
    conv(nslice - 1)
    acc = acc_ref[...]
    if final:
        msf = jnp.mean(acc * acc, axis=-1, keepdims=True)
        acc = acc * lax.rsqrt(msf + EPS) * fnw_ref[...]
    if no == 1:
        o_refs[0][...] = acc
    else:
        b = jnp.maximum(i - 1, 0)
        first = 0
        for o_ref, end in zip(o_refs, out_ends):
            @pl.when(jnp.logical_and(b >= first, b < end))
            def _(o_ref=o_ref):
                o_ref[...] = acc
            first = end


def _mix(xs, yf, yb, z, h, gnw, dww, dwb, lnw, lnb, wout, n2w, w1, w2, fnw,
         tm, first_blocks, last_blocks, final, out_rows):
    t = sum(a.shape[0] for a in xs)
    nb = t // tm
    outs = [jax.ShapeDtypeStruct((r, D_MODEL), F32) for r in out_rows]
    hb = tm // HALO
    nh = t // HALO
    const = lambda i: (0, 0)
    done = lambda i: (jnp.maximum(i - 1, 0), 0)
    nxt_tile = lambda i: (jnp.minimum(i, nb - 1), 0)
    prev = lambda i: (jnp.maximum(jnp.minimum(i, nb - 1) * hb - 1, 0), 0)
    nxt = lambda i: (jnp.minimum((jnp.minimum(i, nb - 1) + 1) * hb, nh - 1), 0)
    once = pl.Buffered(1)
    nslice = tm // CONV_ROWS
    fs = D_FF // nslice
    w1 = w1.reshape(D_MODEL, nslice, fs).transpose(1, 0, 2).reshape(nslice * D_MODEL, fs)
    kern = functools.partial(_mix_kernel, tm=tm, nb=nb, ends=_stream_ends(xs, tm),
                             out_ends=_stream_ends(outs, tm), first_blocks=first_blocks,
                             last_blocks=last_blocks, final=final)
    return pl.pallas_call(
        kern,
        grid=(nb + 1,),
        in_specs=_stream_specs(xs, tm, lambda i: jnp.maximum(i - 1, 0)) + [
            pl.BlockSpec((tm, D_SSD), done),
            pl.BlockSpec((tm, D_SSD), done),
            pl.BlockSpec((tm, D_SSD), done),
            pl.BlockSpec((tm, D_CM), nxt_tile),
            pl.BlockSpec((HALO, D_CM), prev),
            pl.BlockSpec((HALO, D_CM), nxt),
            pl.BlockSpec((1, D_SSD), const),
            pl.BlockSpec((CM_CONV, D_CM // LANES, 2 * SUBLANES, LANES), lambda i: (0, 0, 0, 0)),
            pl.BlockSpec((SUBLANES, D_CM), const),
            pl.BlockSpec((SUBLANES, D_CM), const),
            pl.BlockSpec((SUBLANES, D_CM), const),
            pl.BlockSpec((D_SSD + D_CM, D_MODEL), const, pipeline_mode=once),
            pl.BlockSpec((1, D_MODEL), const),
            pl.BlockSpec((nslice * D_MODEL, fs), const, pipeline_mode=once),
            pl.BlockSpec((D_FF, D_MODEL), const, pipeline_mode=once),
            pl.BlockSpec((1, D_MODEL), const),
        ],
        out_specs=_stream_specs(outs, tm, lambda i: jnp.maximum(i - 1, 0)),
        out_shape=outs,
        scratch_shapes=[
            pltpu.VMEM((D_CM // LANES, tm + 2 * HALO, LANES), jnp.uint32),
            pltpu.VMEM((D_CM // LANES, tm, LANES), F32),
            pltpu.VMEM((tm, D_MODEL), BF16),
            pltpu.VMEM((tm, D_MODEL), F32),
            pltpu.VMEM((2, tm, fs), BF16),
        ],
        compiler_params=pltpu.CompilerParams(
            dimension_semantics=("arbitrary",), vmem_limit_bytes=VMEM_LIMIT),
        name="mix_mlp",
    )(*xs, yf, yb, z, h, h, h, gnw, dww, dwb, lnw, lnb, wout, n2w, w1, w2, fnw)


def _block_size(seq_lens, want):
    b = want
    while any(s % b for s in seq_lens):
        b //= 2
    assert b >= CHUNK, "sequence lengths must be multiples of the SSD chunk"
    return b


def kernel(x_prompt, x_sample, norm1_w, w_in, ssd_conv_w, ssd_conv_b, dt_bias, a_log, d_skip,
           ssd_norm_w, cm_dw_w, cm_dw_b, cm_ln_w, cm_ln_b, w_out, norm2_w, w_mlp_in, w_mlp_out,
           final_norm_w):
    depth = w_in.shape[0]
    seq_lens = [x_prompt.shape[1]] * x_prompt.shape[0] + [x_sample.shape[1]] * x_sample.shape[0]
    n_prompt = x_prompt.shape[0] * x_prompt.shape[1]
    xs = (x_prompt.reshape(-1, D_MODEL), x_sample.reshape(-1, D_MODEL))

    tb = _block_size(seq_lens, 512)
    starts = np.concatenate([[0], np.cumsum(seq_lens)[:-1]])
    ends = np.cumsum(seq_lens)
    first_blocks = tuple(int(s) // tb for s in starts)
    last_blocks = tuple(int(e) // tb - 1 for e in ends)

    s1 = D_SSD
    s2 = s1 + D_XBC
    s3 = s2 + 2 * N_HEADS
    row = lambda v: v.reshape(1, -1).astype(F32)
    for l in range(depth):
        wl = w_in[l]
        wz = wl[:, :s1].astype(BF16)
        wx = jnp.concatenate([wl[:, s1:s3], jnp.zeros((D_MODEL, DT_PAD - 2 * N_HEADS), F32)],
                             axis=1).astype(BF16)
        wcm = wl[:, s3:].astype(BF16)
        dtb = jnp.pad(dt_bias[l].reshape(1, -1), ((0, 0), (0, DT_PAD - 2 * N_HEADS)))
        alog = jnp.pad(a_log[l].reshape(1, -1), ((0, 0), (0, DT_PAD - 2 * N_HEADS)))
        z, xbc, dt, h = _inproj(xs, row(norm1_w[l]), wz, wx, wcm, dtb, tb)

        rep = lambda v: jnp.broadcast_to(v.reshape(1, -1).astype(F32), (SUBLANES, v.size))
        convw = jnp.broadcast_to(
            ssd_conv_w[l].astype(BF16).reshape(SSD_CONV, D_XBC // LANES, 1, LANES),
            (SSD_CONV, D_XBC // LANES, 2 * SUBLANES, LANES))
        drow = jnp.repeat(d_skip[l], HEAD_DIM).reshape(1, -1)
        yf, act = _ssd_fwd(xbc, dt, convw, rep(ssd_conv_b[l]), alog, drow, tb,
                           first_blocks, last_blocks)
        yb = _ssd_bwd(act, dt, alog, tb, last_blocks)

        dww = jnp.broadcast_to(cm_dw_w[l].astype(BF16).reshape(CM_CONV, D_CM // LANES, 1, LANES),
                               (CM_CONV, D_CM // LANES, 2 * SUBLANES, LANES))
        last = l == depth - 1
        n_sample = x_sample.shape[0] * x_sample.shape[1]
        out_rows = (n_prompt, n_sample) if last else (n_prompt + n_sample,)
        x = _mix(xs, yf, yb, z, h, row(ssd_norm_w[l]), dww, rep(cm_dw_b[l]), rep(cm_ln_w[l]),
                 rep(cm_ln_b[l]), w_out[l].astype(BF16), row(norm2_w[l]),
                 w_mlp_in[l].astype(BF16), w_mlp_out[l].astype(BF16), row(final_norm_w),
                 tb, first_blocks, last_blocks, last, out_rows)
        xs = tuple(x)

    y_prompt, y_sample = xs
    return y_prompt.reshape(x_prompt.shape), y_sample.reshape(x_sample.shape)
```

```python
import functools

import jax
import jax.numpy as jnp
import numpy as np
from jax import lax
from jax.experimental import pallas as pl
from jax.experimental.pallas import tpu as pltpu

D_MODEL = 1024
D_SSD = 1024
HEAD_DIM = 64
N_HEADS = 16
N_GROUPS = 2
D_STATE = 128
CHUNK = 128
D_XBC = D_SSD + 2 * N_GROUPS * D_STATE
D_CM = 1024
SSD_CONV = 5
CM_CONV = 31
D_FF = 4096
EPS = 1e-5

LANES = 128
SUBLANES = 8
HALO = 16
DT_PAD = LANES
VMEM_LIMIT = 56 * 1024 * 1024

F32 = jnp.float32
BF16 = jnp.bfloat16


def _dot(a, b):
    return jnp.dot(a, b, preferred_element_type=F32)


def _sigmoid(x):
    return 1.0 / (1.0 + jnp.exp(-x))


def _any_eq(j, values):
    return functools.reduce(jnp.logical_or, [j == v for v in values])


def _stream_specs(xs, tm, tile_of):
    specs, first = [], 0
    for a in xs:
        n = a.shape[0] // tm
        specs.append(pl.BlockSpec(
            (tm, D_MODEL), lambda i, first=first, n=n: (jnp.clip(tile_of(i) - first, 0, n - 1), 0)))
        first += n
    return specs


def _stream_tile(x_refs, ends, tile):
    x = x_refs[-1][...]
    for ref, end in reversed(list(zip(x_refs[:-1], ends[:-1]))):
        x = jnp.where(tile < end, ref[...], x)
    return x


def _stream_ends(xs, tm):
    return tuple(int(v) for v in np.cumsum([a.shape[0] // tm for a in xs]))


def _inproj_kernel(*refs, ends):
    nx = len(ends)
    nw_ref, wz_ref, wx_ref, wcm_ref, dtb_ref, z_ref, xbc_ref, dt_ref, h_ref = refs[nx:]
    x = _stream_tile(refs[:nx], ends, pl.program_id(0))
    ms = jnp.mean(x * x, axis=-1, keepdims=True)
    n = (x * lax.rsqrt(ms + EPS) * nw_ref[...]).astype(BF16)
    z_ref[...] = _dot(n, wz_ref[...]).astype(BF16)
    xd = _dot(n, wx_ref[...])
    xbc_ref[...] = xd[:, :D_XBC].astype(BF16)
    v = xd[:, D_XBC:] + dtb_ref[...]
    dt_ref[...] = jnp.maximum(v, 0.0) + jnp.log(1.0 + jnp.exp(-jnp.abs(v)))
    nt = 2 * LANES
    for c in range(D_CM // nt):
        cols = slice(c * nt, (c + 1) * nt)
        gate = slice(D_CM + c * nt, D_CM + (c + 1) * nt)
        h_ref[:, cols] = (_dot(n, wcm_ref[:, cols]) * _sigmoid(_dot(n, wcm_ref[:, gate]))).astype(BF16)


def _inproj(xs, nw, wz, wx, wcm, dtb, tm):
    t = sum(a.shape[0] for a in xs)
    const = lambda i: (0, 0)
    row = lambda i: (i, 0)
    once = pl.Buffered(1)
    return pl.pallas_call(
        functools.partial(_inproj_kernel, ends=_stream_ends(xs, tm)),
        grid=(t // tm,),
        in_specs=_stream_specs(xs, tm, lambda i: i) + [
            pl.BlockSpec((1, D_MODEL), const),
            pl.BlockSpec((D_MODEL, D_SSD), const, pipeline_mode=once),
            pl.BlockSpec((D_MODEL, D_XBC + DT_PAD), const, pipeline_mode=once),
            pl.BlockSpec((D_MODEL, 2 * D_CM), const, pipeline_mode=once),
            pl.BlockSpec((1, DT_PAD), const),
        ],
        out_specs=[
            pl.BlockSpec((tm, D_SSD), row),
            pl.BlockSpec((tm, D_XBC), row),
            pl.BlockSpec((tm, DT_PAD), row),
            pl.BlockSpec((tm, D_CM), row),
        ],
        out_shape=[
            jax.ShapeDtypeStruct((t, D_SSD), BF16),
            jax.ShapeDtypeStruct((t, D_XBC), BF16),
            jax.ShapeDtypeStruct((t, DT_PAD), F32),
            jax.ShapeDtypeStruct((t, D_CM), BF16),
        ],
        compiler_params=pltpu.CompilerParams(
            dimension_semantics=("arbitrary",), vmem_limit_bytes=VMEM_LIMIT),
        name="inproj",
    )(*xs, nw, wz, wx, wcm, dtb)


SSD_CONV_ROWS = 64


def _pair_words(win, pbuf, nrow):
    nword = nrow // 2
    for c in range(win.shape[1] // LANES):
        even = pltpu.bitcast(win[:, c * LANES:(c + 1) * LANES], jnp.uint32)
        nxt = pltpu.roll(even, nword - 1, axis=0)
        odd = (even >> 16) | (nxt << 16)
        pbuf[c, pl.ds(0, nword, stride=2), :] = even
        pbuf[c, pl.ds(1, nword, stride=2), :] = odd


def _scan_mask(reverse):
    li = lax.broadcasted_iota(jnp.int32, (CHUNK, CHUNK), 0)
    si = lax.broadcasted_iota(jnp.int32, (CHUNK, CHUNK), 1)
    return (si >= li) if reverse else (si <= li)


def _ssd_prepare(c, act, dt, a_row, sc, reverse):
    acs_s, acst_s, w2t_s, bgt_s, cb_s = sc
    a = dt * a_row
    tmat = jnp.where(_scan_mask(reverse), 1.0, 0.0).astype(BF16)
    a1 = a.astype(BF16)
    rem = a - a1.astype(F32)
    a2 = rem.astype(BF16)
    a3 = (rem - a2.astype(F32)).astype(BF16)
    parts = _dot(tmat, jnp.concatenate([a1, a2, a3], axis=1))
    acs = parts[:, :DT_PAD] + parts[:, DT_PAD:2 * DT_PAD] + parts[:, 2 * DT_PAD:]
    tot = acs[0:1, :] if reverse else acs[CHUNK - 1:CHUNK, :]
    w2 = dt * jnp.exp(tot - acs)
    acs_s[c * CHUNK:(c + 1) * CHUNK, :] = acs
    acst_s[c] = (acs - jnp.log(dt)).T
    w2t_s[c] = w2.T
    for g in range(N_GROUPS):
        bg = act[:, D_SSD + g * D_STATE:D_SSD + (g + 1) * D_STATE]
        cg = act[:, D_SSD + N_GROUPS * D_STATE + g * D_STATE:
                 D_SSD + N_GROUPS * D_STATE + (g + 1) * D_STATE]
        cb_s[c, g] = lax.dot_general(cg.astype(BF16), bg.astype(BF16), (((1,), (1,)), ((), ())),
                                     preferred_element_type=F32)
        bgt_s[c, g] = bg.T


def _ssd_scan_chunk(c, act_ref, sc, h_ref, y_ref, drow, reverse):
    acs_s, acst_s, w2t_s, bgt_s, cb_s = sc
    rows = pl.ds(pl.multiple_of(c * CHUNK, CHUNK), CHUNK)
    acs = acs_s[rows, :]
    acs_t = acst_s[c]
    w2_t = w2t_s[c]
    mask = _scan_mask(reverse)
    lane_lo = lax.broadcasted_iota(jnp.int32, (CHUNK, LANES), 1) < HEAD_DIM
    hoff = N_HEADS if reverse else 0
    heads_per_group = N_HEADS // N_GROUPS
    gw = heads_per_group * HEAD_DIM
    c0 = D_SSD + N_GROUPS * D_STATE
    for g in range(N_GROUPS):
        cb = cb_s[c, g]
        bg_t = bgt_s[c, g]
        cg_bf = act_ref[rows, c0 + g * D_STATE:c0 + (g + 1) * D_STATE]
        for pp in range(heads_per_group // 2):
            lanes = slice(g * gw + pp * LANES, g * gw + (pp + 1) * LANES)
            mats = []
            for hh in range(2):
                hl = hoff + g * heads_per_group + 2 * pp + hh
                colb = jnp.broadcast_to(acs[:, hl:hl + 1], (CHUNK, CHUNK))
                rowb = jnp.broadcast_to(acs_t[hl:hl + 1, :], (CHUNK, CHUNK))
                m = cb * jnp.exp(jnp.where(mask, colb - rowb, -1e30))
                bw = bg_t * jnp.broadcast_to(w2_t[hl:hl + 1, :], (CHUNK, CHUNK))
                mats.append((colb, m.astype(BF16), bw.astype(BF16)))
            (colb0, m0, bw0), (colb1, m1, bw1) = mats
            lhs = jnp.concatenate([jnp.concatenate([m0, m1], axis=1),
                                   jnp.concatenate([bw0, bw1], axis=1)], axis=0)
            xp = act_ref[rows, lanes]
            zero = jnp.zeros_like(xp)
            rhs = jnp.concatenate([jnp.where(lane_lo, xp, zero),
                                   jnp.where(lane_lo, zero, xp)], axis=0)
            out = _dot(lhs, rhs)
            ep = jnp.exp(jnp.where(lane_lo, colb0, colb1))
            yo = _dot(cg_bf, h_ref[:, lanes].astype(BF16))
            y = out[:CHUNK] + yo * ep
            if not reverse:
                y = y + xp.astype(F32) * drow[:, lanes]
            y_ref[rows, lanes] = y.astype(y_ref.dtype)
            dec = ep[0:1, :] if reverse else ep[CHUNK - 1:CHUNK, :]
            h_ref[:, lanes] = h_ref[:, lanes] * dec + out[CHUNK:]


def _scan_scratch(tb):
    nchunk = tb // CHUNK
    return [
        pltpu.VMEM((tb, DT_PAD), F32),
        pltpu.VMEM((nchunk, DT_PAD, CHUNK), F32),
        pltpu.VMEM((nchunk, DT_PAD, CHUNK), F32),
        pltpu.VMEM((nchunk, N_GROUPS, D_STATE, CHUNK), F32),
        pltpu.VMEM((nchunk, N_GROUPS, CHUNK, CHUNK), F32),
    ]


def _ssd_fwd_kernel(x_ref, xp_ref, xn_ref, dt_ref, convw_ref, convb_ref, alog_ref, drow_ref,
                    y_ref, act_ref, pbuf, cs, *rest, tb, first_blocks, last_blocks):
    sc, h_ref = rest[:-1], rest[-1]
    i = pl.program_id(0)
    is_first = _any_eq(i, first_blocks)
    is_last = _any_eq(i, last_blocks)
    ncol = D_XBC // LANES
    nchunk = tb // CHUNK
    zero = jnp.zeros((HALO, D_XBC), BF16)
    win = jnp.concatenate([jnp.where(is_first, zero, xp_ref[...]), x_ref[...],
                           jnp.where(is_last, zero, xn_ref[...])], axis=0)
    _pair_words(win, pbuf, tb + 2 * HALO)

    @pl.when(is_first)
    def _():
        h_ref[...] = jnp.zeros_like(h_ref)

    ngrp = SSD_CONV_ROWS // (2 * SUBLANES)

    def conv_body(j, carry):
        r0 = pl.multiple_of(j * SSD_CONV_ROWS, SSD_CONV_ROWS)
        for c in range(ncol):
            accs = [None] * ngrp
            for k in range(SSD_CONV):
                wk = convw_ref[k, c].astype(F32)
                for g in range(ngrp):
                    row = r0 + 2 * SUBLANES * g + HALO - SSD_CONV // 2 + k
                    words = pbuf[c, pl.ds(row, SUBLANES, stride=2), :]
                    term = pltpu.bitcast(words, BF16).astype(F32) * wk
                    accs[g] = term if accs[g] is None else accs[g] + term
            for g in range(ngrp):
                cs[c, pl.ds(r0 + 2 * SUBLANES * g, 2 * SUBLANES), :] = accs[g]
        return carry

    lax.fori_loop(0, tb // SSD_CONV_ROWS, conv_body, 0)

    a_row = -jnp.exp(alog_ref[...])
    for c in range(nchunk):
        rows = slice(c * CHUNK, (c + 1) * CHUNK)
        pre = jnp.concatenate([cs[ct, rows, :] for ct in range(ncol)], axis=1) + convb_ref[0:1, :]
        act = pre * _sigmoid(pre)
        act_ref[rows, :] = act.astype(act_ref.dtype)
        _ssd_prepare(c, act, dt_ref[rows, :], a_row, sc, False)

    drow = drow_ref[...]

    def body(c, carry):
        _ssd_scan_chunk(c, act_ref, sc, h_ref, y_ref, drow, False)
        return carry

    lax.fori_loop(0, nchunk, body, 0, unroll=True)


def _ssd_bwd_kernel(act_ref, dt_ref, alog_ref, y_ref, *rest, tb, last_blocks):
    sc, h_ref = rest[:-1], rest[-1]
    nb = pl.num_programs(0)
    j = nb - 1 - pl.program_id(0)

    @pl.when(_any_eq(j, last_blocks))
    def _():
        h_ref[...] = jnp.zeros_like(h_ref)

    a_row = -jnp.exp(alog_ref[...])
    nchunk = tb // CHUNK
    for c in range(nchunk):
        rows = slice(c * CHUNK, (c + 1) * CHUNK)
        _ssd_prepare(c, act_ref[rows, :].astype(F32), dt_ref[rows, :], a_row, sc, True)

    def body(ci, carry):
        _ssd_scan_chunk(nchunk - 1 - ci, act_ref, sc, h_ref, y_ref, None, True)
        return carry

    lax.fori_loop(0, nchunk, body, 0, unroll=True)


def _ssd_fwd(xbc, dt, convw, convb, alog, drow, tb, first_blocks, last_blocks):
    t = xbc.shape[0]
    hb = tb // HALO
    nh = t // HALO
    const = lambda i: (0, 0)
    row = lambda i: (i, 0)
    prev = lambda i: (jnp.maximum(i * hb - 1, 0), 0)
    nxt = lambda i: (jnp.minimum((i + 1) * hb, nh - 1), 0)
    kern = functools.partial(_ssd_fwd_kernel, tb=tb, first_blocks=first_blocks,
                             last_blocks=last_blocks)
    return pl.pallas_call(
        kern,
        grid=(t // tb,),
        in_specs=[
            pl.BlockSpec((tb, D_XBC), row),
            pl.BlockSpec((HALO, D_XBC), prev),
            pl.BlockSpec((HALO, D_XBC), nxt),
            pl.BlockSpec((tb, DT_PAD), row),
            pl.BlockSpec((SSD_CONV, D_XBC // LANES, 2 * SUBLANES, LANES), lambda i: (0, 0, 0, 0)),
            pl.BlockSpec((SUBLANES, D_XBC), const),
            pl.BlockSpec((1, DT_PAD), const),
            pl.BlockSpec((1, D_SSD), const),
        ],
        out_specs=[pl.BlockSpec((tb, D_SSD), row), pl.BlockSpec((tb, D_XBC), row)],
        out_shape=[jax.ShapeDtypeStruct((t, D_SSD), BF16), jax.ShapeDtypeStruct((t, D_XBC), BF16)],
        scratch_shapes=[
            pltpu.VMEM((D_XBC // LANES, tb + 2 * HALO, LANES), jnp.uint32),
            pltpu.VMEM((D_XBC // LANES, tb, LANES), F32),
        ] + _scan_scratch(tb) + [pltpu.VMEM((D_STATE, D_SSD), F32)],
        compiler_params=pltpu.CompilerParams(
            dimension_semantics=("arbitrary",), vmem_limit_bytes=VMEM_LIMIT),
        name="ssd_fwd",
    )(xbc, xbc, xbc, dt, convw, convb, alog, drow)


def _ssd_bwd(act, dt, alog, tb, last_blocks):
    t = act.shape[0]
    nb = t // tb
    const = lambda i: (0, 0)
    rev = lambda i: (nb - 1 - i, 0)
    kern = functools.partial(_ssd_bwd_kernel, tb=tb, last_blocks=last_blocks)
    return pl.pallas_call(
        kern,
        grid=(nb,),
        in_specs=[
            pl.BlockSpec((tb, D_XBC), rev),
            pl.BlockSpec((tb, DT_PAD), rev),
            pl.BlockSpec((1, DT_PAD), const),
        ],
        out_specs=pl.BlockSpec((tb, D_SSD), rev),
        out_shape=jax.ShapeDtypeStruct((t, D_SSD), BF16),
        scratch_shapes=_scan_scratch(tb) + [pltpu.VMEM((D_STATE, D_SSD), F32)],
        compiler_params=pltpu.CompilerParams(
            dimension_semantics=("arbitrary",), vmem_limit_bytes=VMEM_LIMIT),
        name="ssd_bwd",
    )(act, dt, alog)


CONV_ROWS = 64


def _conv31_rows(pbuf, dww_ref, cm_ref, r0):
    ngrp = CONV_ROWS // (2 * SUBLANES)
    for c in range(D_CM // LANES):
        accs = [None] * ngrp
        for k in range(CM_CONV):
            wk = dww_ref[k, c].astype(F32)
            for g in range(ngrp):
                words = pbuf[c, pl.ds(r0 + 2 * SUBLANES * g + k + 1, SUBLANES, stride=2), :]
                term = pltpu.bitcast(words, BF16).astype(F32) * wk
                accs[g] = term if accs[g] is None else accs[g] + term
        for g in range(ngrp):
            cm_ref[c, pl.ds(r0 + 2 * SUBLANES * g, 2 * SUBLANES), :] = accs[g]


def _mix_kernel(*refs, tm, nb, ends, out_ends, first_blocks, last_blocks, final):
    nx, no = len(ends), len(out_ends)
    (yf_ref, yb_ref, z_ref, h_ref, hp_ref, hn_ref, gnw_ref, dww_ref, dwb_ref, lnw_ref, lnb_ref,
     wout_ref, n2w_ref, w1_ref, w2_ref, fnw_ref) = refs[nx:nx + 16]
    o_refs = refs[nx + 16:nx + 16 + no]
    pbuf, cm_ref, n2_ref, acc_ref, hm_ref = refs[nx + 16 + no:]
    i = pl.program_id(0)
    a = jnp.minimum(i, nb - 1)
    is_first = _any_eq(a, first_blocks)
    is_last = _any_eq(a, last_blocks)

    @pl.when(i == 0)
    def _():
        cm_ref[...] = jnp.zeros_like(cm_ref)

    zf = z_ref[...].astype(F32)
    y = (yf_ref[...].astype(F32) + yb_ref[...].astype(F32)) * (zf * _sigmoid(zf))
    gw = D_SSD // N_GROUPS
    parts = []
    for g in range(N_GROUPS):
        yg = y[:, g * gw:(g + 1) * gw]
        ms = jnp.mean(yg * yg, axis=-1, keepdims=True)
        parts.append(yg * lax.rsqrt(ms + EPS))
    y_ssd = (jnp.concatenate(parts, axis=1) * gnw_ref[...]).astype(BF16)

    ncol = D_CM // LANES
    x = _stream_tile(refs[:nx], ends, jnp.maximum(i - 1, 0))
    acc_ref[...] = x + _dot(y_ssd, wout_ref[0:D_SSD, :])
    cm = jnp.concatenate([cm_ref[c] for c in range(ncol)], axis=1) + dwb_ref[0:1, :]
    mu = jnp.mean(cm, axis=-1, keepdims=True)
    xc = cm - mu
    var = jnp.mean(xc * xc, axis=-1, keepdims=True)
    ln = xc * lax.rsqrt(var + EPS) * lnw_ref[0:1, :] + lnb_ref[0:1, :]
    y_cm = (ln * _sigmoid(ln)).astype(BF16)
    acc_ref[...] += _dot(y_cm, wout_ref[D_SSD:, :])
    x1 = acc_ref[...]

    ms2 = jnp.mean(x1 * x1, axis=-1, keepdims=True)
    n2_ref[...] = (x1 * lax.rsqrt(ms2 + EPS) * n2w_ref[...]).astype(BF16)

    zero = jnp.zeros((HALO, D_CM), BF16)
    win = jnp.concatenate([jnp.where(is_first, zero, hp_ref[...]), h_ref[...],
                           jnp.where(is_last, zero, hn_ref[...])], axis=0)
    _pair_words(win, pbuf, tm + 2 * HALO)

    fs = w1_ref.shape[1]
    nslice = tm // CONV_ROWS

    def up(j):
        w1j = w1_ref[pl.ds(pl.multiple_of(j * D_MODEL, D_MODEL), D_MODEL), :]
        hmid = jnp.maximum(_dot(n2_ref[...], w1j), 0.0)
        return (hmid * hmid).astype(BF16)

    def down(j, slot):
        w2j = w2_ref[pl.ds(pl.multiple_of(j * fs, fs), fs), :]
        acc_ref[...] += _dot(hm_ref[slot], w2j)

    def conv(j):
        _conv31_rows(pbuf, dww_ref, cm_ref, pl.multiple_of(j * CONV_ROWS, CONV_ROWS))

    hm_ref[0] = up(0)

    def body(jj, carry):
        j = 2 * jj
        hm_ref[1] = up(j + 1)
        down(j, 0)
        conv(j)
        hm_ref[0] = up(j + 2)
        down(j + 1, 1)
        conv(j + 1)
        return carry

    lax.fori_loop(0, nslice // 2 - 1, body, 0)
    hm_ref[1] = up(nslice - 1)
    down(nslice - 2, 0)
    conv(nslice - 2)
    down(nslice - 1, 1)
    conv(nslice - 1)
    acc = acc_ref[...]
    if final:
        msf = jnp.mean(acc * acc, axis=-1, keepdims=True)
        acc = acc * lax.rsqrt(msf + EPS) * fnw_ref[...]
    if no == 1:
        o_refs[0][...] = acc
    else:
        b = jnp.maximum(i - 1, 0)
        first = 0
        for o_ref, end in zip(o_refs, out_ends):
            @pl.when(jnp.logical_and(b >= first, b < end))
            def _(o_ref=o_ref):
                o_ref[...] = acc
            first = end


def _mix(xs, yf, yb, z, h, gnw, dww, dwb, lnw, lnb, wout, n2w, w1, w2, fnw,
         tm, first_blocks, last_blocks, final, out_rows):
    t = sum(a.shape[0] for a in xs)
    nb = t // tm
    outs = [jax.ShapeDtypeStruct((r, D_MODEL), F32) for r in out_rows]
    hb = tm // HALO
    nh = t // HALO
    const = lambda i: (0, 0)
    done = lambda i: (jnp.maximum(i - 1, 0), 0)
    nxt_tile = lambda i: (jnp.minimum(i, nb - 1), 0)
    prev = lambda i: (jnp.maximum(jnp.minimum(i, nb - 1) * hb - 1, 0), 0)
    nxt = lambda i: (jnp.minimum((jnp.minimum(i, nb - 1) + 1) * hb, nh - 1), 0)
    once = pl.Buffered(1)
    nslice = tm // CONV_ROWS
    fs = D_FF // nslice
    w1 = w1.reshape(D_MODEL, nslice, fs).transpose(1, 0, 2).reshape(nslice * D_MODEL, fs)
    kern = functools.partial(_mix_kernel, tm=tm, nb=nb, ends=_stream_ends(xs, tm),
                             out_ends=_stream_ends(outs, tm), first_blocks=first_blocks,
                             last_blocks=last_blocks, final=final)
    return pl.pallas_call(
        kern,
        grid=(nb + 1,),
        in_specs=_stream_specs(xs, tm, lambda i: jnp.maximum(i - 1, 0)) + [
            pl.BlockSpec((tm, D_SSD), done),
            pl.BlockSpec((tm, D_SSD), done),
            pl.BlockSpec((tm, D_SSD), done),
            pl.BlockSpec((tm, D_CM), nxt_tile),
            pl.BlockSpec((HALO, D_CM), prev),
            pl.BlockSpec((HALO, D_CM), nxt),
            pl.BlockSpec((1, D_SSD), const),
            pl.BlockSpec((CM_CONV, D_CM // LANES, 2 * SUBLANES, LANES), lambda i: (0, 0, 0, 0)),
            pl.BlockSpec((SUBLANES, D_CM), const),
            pl.BlockSpec((SUBLANES, D_CM), const),
            pl.BlockSpec((SUBLANES, D_CM), const),
            pl.BlockSpec((D_SSD + D_CM, D_MODEL), const, pipeline_mode=once),
            pl.BlockSpec((1, D_MODEL), const),
            pl.BlockSpec((nslice * D_MODEL, fs), const, pipeline_mode=once),
            pl.BlockSpec((D_FF, D_MODEL), const, pipeline_mode=once),
            pl.BlockSpec((1, D_MODEL), const),
        ],
        out_specs=_stream_specs(outs, tm, lambda i: jnp.maximum(i - 1, 0)),
        out_shape=outs,
        scratch_shapes=[
            pltpu.VMEM((D_CM // LANES, tm + 2 * HALO, LANES), jnp.uint32),
            pltpu.VMEM((D_CM // LANES, tm, LANES), F32),
            pltpu.VMEM((tm, D_MODEL), BF16),
            pltpu.VMEM((tm, D_MODEL), F32),
            pltpu.VMEM((2, tm, fs), BF16),
        ],
        compiler_params=pltpu.CompilerParams(
            dimension_semantics=("arbitrary",), vmem_limit_bytes=VMEM_LIMIT),
        name="mix_mlp",
    )(*xs, yf, yb, z, h, h, h, gnw, dww, dwb, lnw, lnb, wout, n2w, w1, w2, fnw)


def _block_size(seq_lens, want):
    b = want
    while any(s % b for s in seq_lens):
        b //= 2
    assert b >= CHUNK, "sequence lengths must be multiples of the SSD chunk"
    return b


def kernel(x_prompt, x_sample, norm1_w, w_in, ssd_conv_w, ssd_conv_b, dt_bias, a_log, d_skip,
           ssd_norm_w, cm_dw_w, cm_dw_b, cm_ln_w, cm_ln_b, w_out, norm2_w, w_mlp_in, w_mlp_out,
           final_norm_w):
    depth = w_in.shape[0]
    seq_lens = [x_prompt.shape[1]] * x_prompt.shape[0] + [x_sample.shape[1]] * x_sample.shape[0]
    n_prompt = x_prompt.shape[0] * x_prompt.shape[1]
    xs = (x_prompt.reshape(-1, D_MODEL), x_sample.reshape(-1, D_MODEL))

    tb = _block_size(seq_lens, 512)
    starts = np.concatenate([[0], np.cumsum(seq_lens)[:-1]])
    ends = np.cumsum(seq_lens)
    first_blocks = tuple(int(s) // tb for s in starts)
    last_blocks = tuple(int(e) // tb - 1 for e in ends)

    s1 = D_SSD
    s2 = s1 + D_XBC
    s3 = s2 + 2 * N_HEADS
    row = lambda v: v.reshape(1, -1).astype(F32)
    for l in range(depth):
        wl = w_in[l]
        wz = wl[:, :s1].astype(BF16)
        wx = jnp.concatenate([wl[:, s1:s3], jnp.zeros((D_MODEL, DT_PAD - 2 * N_HEADS), F32)],
                             axis=1).astype(BF16)
        wcm = wl[:, s3:].astype(BF16)
        dtb = jnp.pad(dt_bias[l].reshape(1, -1), ((0, 0), (0, DT_PAD - 2 * N_HEADS)))
        alog = jnp.pad(a_log[l].reshape(1, -1), ((0, 0), (0, DT_PAD - 2 * N_HEADS)))
        z, xbc, dt, h = _inproj(xs, row(norm1_w[l]), wz, wx, wcm, dtb, tb)

        rep = lambda v: jnp.broadcast_to(v.reshape(1, -1).astype(F32), (SUBLANES, v.size))
        convw = jnp.broadcast_to(
            ssd_conv_w[l].astype(BF16).reshape(SSD_CONV, D_XBC // LANES, 1, LANES),
            (SSD_CONV, D_XBC // LANES, 2 * SUBLANES, LANES))
        drow = jnp.repeat(d_skip[l], HEAD_DIM).reshape(1, -1)
        yf, act = _ssd_fwd(xbc, dt, convw, rep(ssd_conv_b[l]), alog, drow, tb,
                           first_blocks, last_blocks)
        yb = _ssd_bwd(act, dt, alog, tb, last_blocks)

        dww = jnp.broadcast_to(cm_dw_w[l].astype(BF16).reshape(CM_CONV, D_CM // LANES, 1, LANES),
                               (CM_CONV, D_CM // LANES, 2 * SUBLANES, LANES))
        last = l == depth - 1
        n_sample = x_sample.shape[0] * x_sample.shape[1]
        out_rows = (n_prompt, n_sample) if last else (n_prompt + n_sample,)
        x = _mix(xs, yf, yb, z, h, row(ssd_norm_w[l]), dww, rep(cm_dw_b[l]), rep(cm_ln_w[l]),
                 rep(cm_ln_b[l]), w_out[l].astype(BF16), row(norm2_w[l]),
                 w_mlp_in[l].astype(BF16), w_mlp_out[l].astype(BF16), row(final_norm_w),
                 tb, first_blocks, last_blocks, last, out_rows)
        xs = tuple(x)

    y_prompt, y_sample = xs
    return y_prompt.reshape(x_prompt.shape), y_sample.reshape(x_sample.shape)
```

```python
import functools

import jax
import jax.numpy as jnp
import numpy as np
from jax import lax
from jax.experimental import pallas as pl
from jax.experimental.pallas import tpu as pltpu

D_MODEL = 1024
D_SSD = 1024
HEAD_DIM = 64
N_HEADS = 16
N_GROUPS = 2
D_STATE = 128
CHUNK = 128
D_XBC = D_SSD + 2 * N_GROUPS * D_STATE
D_CM = 1024
SSD_CONV = 5
CM_CONV = 31
D_FF = 4096
EPS = 1e-5

LANES = 128
SUBLANES = 8
HALO = 16
DT_PAD = LANES
MIB = 1024 * 1024
VMEM_LIMIT = 56 * MIB
VMEM_LIMIT_INPROJ = 40 * MIB
VMEM_LIMIT_SCAN_FWD = 32 * MIB
VMEM_LIMIT_SCAN_BWD = 24 * MIB

F32 = jnp.float32
BF16 = jnp.bfloat16


def _dot(a, b):
    return jnp.dot(a, b, preferred_element_type=F32)


def _sigmoid(x):
    return 1.0 / (1.0 + jnp.exp(-x))


def _any_eq(j, values):
    return functools.reduce(jnp.logical_or, [j == v for v in values])


def _stream_specs(xs, tm, tile_of):
    specs, first = [], 0
    for a in xs:
        n = a.shape[0] // tm
        specs.append(pl.BlockSpec(
            (tm, D_MODEL), lambda i, first=first, n=n: (jnp.clip(tile_of(i) - first, 0, n - 1), 0)))
        first += n
    return specs


def _stream_tile(x_refs, ends, tile):
    x = x_refs[-1][...]
    for ref, end in reversed(list(zip(x_refs[:-1], ends[:-1]))):
        x = jnp.where(tile < end, ref[...], x)
    return x


def _stream_ends(xs, tm):
    return tuple(int(v) for v in np.cumsum([a.shape[0] // tm for a in xs]))


def _inproj_kernel(*refs, ends):
    nx = len(ends)
    nw_ref, wz_ref, wx_ref, wcm_ref, dtb_ref, z_ref, xbc_ref, dt_ref, h_ref = refs[nx:]
    x = _stream_tile(refs[:nx], ends, pl.program_id(0))
    ms = jnp.mean(x * x, axis=-1, keepdims=True)
    n = (x * lax.rsqrt(ms + EPS) * nw_ref[...]).astype(BF16)
    z_ref[...] = _dot(n, wz_ref[...]).astype(BF16)
    xd = _dot(n, wx_ref[...])
    xbc_ref[...] = xd[:, :D_XBC].astype(BF16)
    v = xd[:, D_XBC:] + dtb_ref[...]
    dt_ref[...] = jnp.maximum(v, 0.0) + jnp.log(1.0 + jnp.exp(-jnp.abs(v)))
    nt = 2 * LANES
    for c in range(D_CM // nt):
        cols = slice(c * nt, (c + 1) * nt)
        gate = slice(D_CM + c * nt, D_CM + (c + 1) * nt)
        h_ref[:, cols] = (_dot(n, wcm_ref[:, cols]) * _sigmoid(_dot(n, wcm_ref[:, gate]))).astype(BF16)


def _inproj(xs, nw, wz, wx, wcm, dtb, tm):
    t = sum(a.shape[0] for a in xs)
    const = lambda i: (0, 0)
    row = lambda i: (i, 0)
    once = pl.Buffered(1)
    return pl.pallas_call(
        functools.partial(_inproj_kernel, ends=_stream_ends(xs, tm)),
        grid=(t // tm,),
        in_specs=_stream_specs(xs, tm, lambda i: i) + [
            pl.BlockSpec((1, D_MODEL), const),
            pl.BlockSpec((D_MODEL, D_SSD), const, pipeline_mode=once),
            pl.BlockSpec((D_MODEL, D_XBC + DT_PAD), const, pipeline_mode=once),
            pl.BlockSpec((D_MODEL, 2 * D_CM), const, pipeline_mode=once),
            pl.BlockSpec((1, DT_PAD), const),
        ],
        out_specs=[
            pl.BlockSpec((tm, D_SSD), row),
            pl.BlockSpec((tm, D_XBC), row),
            pl.BlockSpec((tm, DT_PAD), row),
            pl.BlockSpec((tm, D_CM), row),
        ],
        out_shape=[
            jax.ShapeDtypeStruct((t, D_SSD), BF16),
            jax.ShapeDtypeStruct((t, D_XBC), BF16),
            jax.ShapeDtypeStruct((t, DT_PAD), F32),
            jax.ShapeDtypeStruct((t, D_CM), BF16),
        ],
        compiler_params=pltpu.CompilerParams(
            dimension_semantics=("arbitrary",), vmem_limit_bytes=VMEM_LIMIT_INPROJ),
        name="inproj",
    )(*xs, nw, wz, wx, wcm, dtb)


SSD_CONV_ROWS = 64


def _pair_words(win, pbuf, nrow):
    nword = nrow // 2
    for c in range(win.shape[1] // LANES):
        even = pltpu.bitcast(win[:, c * LANES:(c + 1) * LANES], jnp.uint32)
        nxt = pltpu.roll(even, nword - 1, axis=0)
        odd = (even >> 16) | (nxt << 16)
        pbuf[c, pl.ds(0, nword, stride=2), :] = even
        pbuf[c, pl.ds(1, nword, stride=2), :] = odd


def _scan_mask(reverse):
    li = lax.broadcasted_iota(jnp.int32, (CHUNK, CHUNK), 0)
    si = lax.broadcasted_iota(jnp.int32, (CHUNK, CHUNK), 1)
    return (si >= li) if reverse else (si <= li)


def _ssd_prepare(c, act, dt, a_row, sc, reverse):
    acs_s, acst_s, w2t_s, bgt_s, cb_s = sc
    a = dt * a_row
    tmat = jnp.where(_scan_mask(reverse), 1.0, 0.0).astype(BF16)
    a1 = a.astype(BF16)
    rem = a - a1.astype(F32)
    a2 = rem.astype(BF16)
    a3 = (rem - a2.astype(F32)).astype(BF16)
    parts = _dot(tmat, jnp.concatenate([a1, a2, a3], axis=1))
    acs = parts[:, :DT_PAD] + parts[:, DT_PAD:2 * DT_PAD] + parts[:, 2 * DT_PAD:]
    tot = acs[0:1, :] if reverse else acs[CHUNK - 1:CHUNK, :]
    w2 = dt * jnp.exp(tot - acs)
    acs_s[c * CHUNK:(c + 1) * CHUNK, :] = acs
    acst_s[c] = (acs - jnp.log(dt)).T
    w2t_s[c] = w2.T
    for g in range(N_GROUPS):
        bg = act[:, D_SSD + g * D_STATE:D_SSD + (g + 1) * D_STATE]
        cg = act[:, D_SSD + N_GROUPS * D_STATE + g * D_STATE:
                 D_SSD + N_GROUPS * D_STATE + (g + 1) * D_STATE]
        cb_s[c, g] = lax.dot_general(cg.astype(BF16), bg.astype(BF16), (((1,), (1,)), ((), ())),
                                     preferred_element_type=F32)
        bgt_s[c, g] = bg.T


def _ssd_scan_chunk(c, act_ref, sc, h_ref, y_ref, drow, reverse):
    acs_s, acst_s, w2t_s, bgt_s, cb_s = sc
    rows = pl.ds(pl.multiple_of(c * CHUNK, CHUNK), CHUNK)
    acs = acs_s[rows, :]
    acs_t = acst_s[c]
    w2_t = w2t_s[c]
    mask = _scan_mask(reverse)
    lane_lo = lax.broadcasted_iota(jnp.int32, (CHUNK, LANES), 1) < HEAD_DIM
    hoff = N_HEADS if reverse else 0
    heads_per_group = N_HEADS // N_GROUPS
    gw = heads_per_group * HEAD_DIM
    c0 = D_SSD + N_GROUPS * D_STATE
    for g in range(N_GROUPS):
        cb = cb_s[c, g]
        bg_t = bgt_s[c, g]
        cg_bf = act_ref[rows, c0 + g * D_STATE:c0 + (g + 1) * D_STATE]
        for pp in range(heads_per_group // 2):
            lanes = slice(g * gw + pp * LANES, g * gw + (pp + 1) * LANES)
            mats = []
            for hh in range(2):
                hl = hoff + g * heads_per_group + 2 * pp + hh
                colb = jnp.broadcast_to(acs[:, hl:hl + 1], (CHUNK, CHUNK))
                rowb = jnp.broadcast_to(acs_t[hl:hl + 1, :], (CHUNK, CHUNK))
                m = cb * jnp.exp(jnp.where(mask, colb - rowb, -1e30))
                bw = bg_t * jnp.broadcast_to(w2_t[hl:hl + 1, :], (CHUNK, CHUNK))
                mats.append((colb, m.astype(BF16), bw.astype(BF16)))
            (colb0, m0, bw0), (colb1, m1, bw1) = mats
            lhs = jnp.concatenate([jnp.concatenate([m0, m1], axis=1),
                                   jnp.concatenate([bw0, bw1], axis=1)], axis=0)
            xp = act_ref[rows, lanes]
            zero = jnp.zeros_like(xp)
            rhs = jnp.concatenate([jnp.where(lane_lo, xp, zero),
                                   jnp.where(lane_lo, zero, xp)], axis=0)
            out = _dot(lhs, rhs)
            ep = jnp.exp(jnp.where(lane_lo, colb0, colb1))
            yo = _dot(cg_bf, h_ref[:, lanes].astype(BF16))
            y = out[:CHUNK] + yo * ep
            if not reverse:
                y = y + xp.astype(F32) * drow[:, lanes]
            y_ref[rows, lanes] = y.astype(y_ref.dtype)
            dec = ep[0:1, :] if reverse else ep[CHUNK - 1:CHUNK, :]
            h_ref[:, lanes] = h_ref[:, lanes] * dec + out[CHUNK:]


def _scan_scratch(tb):
    nchunk = tb // CHUNK
    return [
        pltpu.VMEM((tb, DT_PAD), F32),
        pltpu.VMEM((nchunk, DT_PAD, CHUNK), F32),
        pltpu.VMEM((nchunk, DT_PAD, CHUNK), F32),
        pltpu.VMEM((nchunk, N_GROUPS, D_STATE, CHUNK), F32),
        pltpu.VMEM((nchunk, N_GROUPS, CHUNK, CHUNK), F32),
    ]


def _ssd_fwd_kernel(x_ref, xp_ref, xn_ref, dt_ref, convw_ref, convb_ref, alog_ref, drow_ref,
                    y_ref, act_ref, pbuf, cs, *rest, tb, first_blocks, last_blocks):
    sc, h_ref = rest[:-1], rest[-1]
    i = pl.program_id(0)
    is_first = _any_eq(i, first_blocks)
    is_last = _any_eq(i, last_blocks)
    ncol = D_XBC // LANES
    nchunk = tb // CHUNK
    zero = jnp.zeros((HALO, D_XBC), BF16)
    win = jnp.concatenate([jnp.where(is_first, zero, xp_ref[...]), x_ref[...],
                           jnp.where(is_last, zero, xn_ref[...])], axis=0)
    _pair_words(win, pbuf, tb + 2 * HALO)

    @pl.when(is_first)
    def _():
        h_ref[...] = jnp.zeros_like(h_ref)

    ngrp = SSD_CONV_ROWS // (2 * SUBLANES)

    def conv_body(j, carry):
        r0 = pl.multiple_of(j * SSD_CONV_ROWS, SSD_CONV_ROWS)
        for c in range(ncol):
            accs = [None] * ngrp
            for k in range(SSD_CONV):
                wk = convw_ref[k, c].astype(F32)
                for g in range(ngrp):
                    row = r0 + 2 * SUBLANES * g + HALO - SSD_CONV // 2 + k
                    words = pbuf[c, pl.ds(row, SUBLANES, stride=2), :]
                    term = pltpu.bitcast(words, BF16).astype(F32) * wk
                    accs[g] = term if accs[g] is None else accs[g] + term
            for g in range(ngrp):
                cs[c, pl.ds(r0 + 2 * SUBLANES * g, 2 * SUBLANES), :] = accs[g]
        return carry

    lax.fori_loop(0, tb // SSD_CONV_ROWS, conv_body, 0)

    a_row = -jnp.exp(alog_ref[...])
    for c in range(nchunk):
        rows = slice(c * CHUNK, (c + 1) * CHUNK)
        pre = jnp.concatenate([cs[ct, rows, :] for ct in range(ncol)], axis=1) + convb_ref[0:1, :]
        act = pre * _sigmoid(pre)
        act_ref[rows, :] = act.astype(act_ref.dtype)
        _ssd_prepare(c, act, dt_ref[rows, :], a_row, sc, False)

    drow = drow_ref[...]

    def body(c, carry):
        _ssd_scan_chunk(c, act_ref, sc, h_ref, y_ref, drow, False)
        return carry

    lax.fori_loop(0, nchunk, body, 0, unroll=True)


def _ssd_bwd_kernel(act_ref, dt_ref, alog_ref, y_ref, *rest, tb, last_blocks):
    sc, h_ref = rest[:-1], rest[-1]
    nb = pl.num_programs(0)
    j = nb - 1 - pl.program_id(0)

    @pl.when(_any_eq(j, last_blocks))
    def _():
        h_ref[...] = jnp.zeros_like(h_ref)

    a_row = -jnp.exp(alog_ref[...])
    nchunk = tb // CHUNK
    for c in range(nchunk):
        rows = slice(c * CHUNK, (c + 1) * CHUNK)
        _ssd_prepare(c, act_ref[rows, :].astype(F32), dt_ref[rows, :], a_row, sc, True)

    def body(ci, carry):
        _ssd_scan_chunk(nchunk - 1 - ci, act_ref, sc, h_ref, y_ref, None, True)
        return carry

    lax.fori_loop(0, nchunk, body, 0, unroll=True)


def _ssd_fwd(xbc, dt, convw, convb, alog, drow, tb, first_blocks, last_blocks):
    t = xbc.shape[0]
    hb = tb // HALO
    nh = t // HALO
    const = lambda i: (0, 0)
    row = lambda i: (i, 0)
    prev = lambda i: (jnp.maximum(i * hb - 1, 0), 0)
    nxt = lambda i: (jnp.minimum((i + 1) * hb, nh - 1), 0)
    kern = functools.partial(_ssd_fwd_kernel, tb=tb, first_blocks=first_blocks,
                             last_blocks=last_blocks)
    return pl.pallas_call(
        kern,
        grid=(t // tb,),
        in_specs=[
            pl.BlockSpec((tb, D_XBC), row),
            pl.BlockSpec((HALO, D_XBC), prev),
            pl.BlockSpec((HALO, D_XBC), nxt),
            pl.BlockSpec((tb, DT_PAD), row),
            pl.BlockSpec((SSD_CONV, D_XBC // LANES, 2 * SUBLANES, LANES), lambda i: (0, 0, 0, 0)),
            pl.BlockSpec((SUBLANES, D_XBC), const),
            pl.BlockSpec((1, DT_PAD), const),
            pl.BlockSpec((1, D_SSD), const),
        ],
        out_specs=[pl.BlockSpec((tb, D_SSD), row), pl.BlockSpec((tb, D_XBC), row)],
        out_shape=[jax.ShapeDtypeStruct((t, D_SSD), BF16), jax.ShapeDtypeStruct((t, D_XBC), BF16)],
        scratch_shapes=[
            pltpu.VMEM((D_XBC // LANES, tb + 2 * HALO, LANES), jnp.uint32),
            pltpu.VMEM((D_XBC // LANES, tb, LANES), F32),
        ] + _scan_scratch(tb) + [pltpu.VMEM((D_STATE, D_SSD), F32)],
        compiler_params=pltpu.CompilerParams(
            dimension_semantics=("arbitrary",), vmem_limit_bytes=VMEM_LIMIT_SCAN_FWD),
        name="ssd_fwd",
    )(xbc, xbc, xbc, dt, convw, convb, alog, drow)


def _ssd_bwd(act, dt, alog, tb, last_blocks):
    t = act.shape[0]
    nb = t // tb
    const = lambda i: (0, 0)
    rev = lambda i: (nb - 1 - i, 0)
    kern = functools.partial(_ssd_bwd_kernel, tb=tb, last_blocks=last_blocks)
    return pl.pallas_call(
        kern,
        grid=(nb,),
        in_specs=[
            pl.BlockSpec((tb, D_XBC), rev),
            pl.BlockSpec((tb, DT_PAD), rev),
            pl.BlockSpec((1, DT_PAD), const),
        ],
        out_specs=pl.BlockSpec((tb, D_SSD), rev),
        out_shape=jax.ShapeDtypeStruct((t, D_SSD), BF16),
        scratch_shapes=_scan_scratch(tb) + [pltpu.VMEM((D_STATE, D_SSD), F32)],
        compiler_params=pltpu.CompilerParams(
            dimension_semantics=("arbitrary",), vmem_limit_bytes=VMEM_LIMIT_SCAN_BWD),
        name="ssd_bwd",
    )(act, dt, alog)


CONV_ROWS = 64


def _conv31_rows(pbuf, dww_ref, cm_ref, r0):
    ngrp = CONV_ROWS // (2 * SUBLANES)
    for c in range(D_CM // LANES):
        accs = [None] * ngrp
        for k in range(CM_CONV):
            wk = dww_ref[k, c].astype(F32)
            for g in range(ngrp):
                words = pbuf[c, pl.ds(r0 + 2 * SUBLANES * g + k + 1, SUBLANES, stride=2), :]
                term = pltpu.bitcast(words, BF16).astype(F32) * wk
                accs[g] = term if accs[g] is None else accs[g] + term
        for g in range(ngrp):
            cm_ref[c, pl.ds(r0 + 2 * SUBLANES * g, 2 * SUBLANES), :] = accs[g]


def _mix_kernel(*refs, tm, nb, ends, out_ends, first_blocks, last_blocks, final):
    nx, no = len(ends), len(out_ends)
    (yf_ref, yb_ref, z_ref, h_ref, hp_ref, hn_ref, gnw_ref, dww_ref, dwb_ref, lnw_ref, lnb_ref,
     wout_ref, n2w_ref, w1_ref, w2_ref, fnw_ref) = refs[nx:nx + 16]
    o_refs = refs[nx + 16:nx + 16 + no]
    pbuf, cm_ref, n2_ref, acc_ref, hm_ref = refs[nx + 16 + no:]
    i = pl.program_id(0)
    a = jnp.minimum(i, nb - 1)
    is_first = _any_eq(a, first_blocks)
    is_last = _any_eq(a, last_blocks)

    @pl.when(i == 0)
    def _():
        cm_ref[...] = jnp.zeros_like(cm_ref)

    zf = z_ref[...].astype(F32)
    y = (yf_ref[...].astype(F32) + yb_ref[...].astype(F32)) * (zf * _sigmoid(zf))
    gw = D_SSD // N_GROUPS
    parts = []
    for g in range(N_GROUPS):
        yg = y[:, g * gw:(g + 1) * gw]
        ms = jnp.mean(yg * yg, axis=-1, keepdims=True)
        parts.append(yg * lax.rsqrt(ms + EPS))
    y_ssd = (jnp.concatenate(parts, axis=1) * gnw_ref[...]).astype(BF16)

    ncol = D_CM // LANES
    x = _stream_tile(refs[:nx], ends, jnp.maximum(i - 1, 0))
    acc_ref[...] = x + _dot(y_ssd, wout_ref[0:D_SSD, :])
    cm = jnp.concatenate([cm_ref[c] for c in range(ncol)], axis=1) + dwb_ref[0:1, :]
    mu = jnp.mean(cm, axis=-1, keepdims=True)
    xc = cm - mu
    var = jnp.mean(xc * xc, axis=-1, keepdims=True)
    ln = xc * lax.rsqrt(var + EPS) * lnw_ref[0:1, :] + lnb_ref[0:1, :]
    y_cm = (ln * _sigmoid(ln)).astype(BF16)
    acc_ref[...] += _dot(y_cm, wout_ref[D_SSD:, :])
    x1 = acc_ref[...]

    ms2 = jnp.mean(x1 * x1, axis=-1, keepdims=True)
    n2_ref[...] = (x1 * lax.rsqrt(ms2 + EPS) * n2w_ref[...]).astype(BF16)

    zero = jnp.zeros((HALO, D_CM), BF16)
    win = jnp.concatenate([jnp.where(is_first, zero, hp_ref[...]), h_ref[...],
                           jnp.where(is_last, zero, hn_ref[...])], axis=0)
    _pair_words(win, pbuf, tm + 2 * HALO)

    fs = w1_ref.shape[1]
    nslice = tm // CONV_ROWS

    def up(j):
        w1j = w1_ref[pl.ds(pl.multiple_of(j * D_MODEL, D_MODEL), D_MODEL), :]
        hmid = jnp.maximum(_dot(n2_ref[...], w1j), 0.0)
        return (hmid * hmid).astype(BF16)

    def down(j, slot):
        w2j = w2_ref[pl.ds(pl.multiple_of(j * fs, fs), fs), :]
        acc_ref[...] += _dot(hm_ref[slot], w2j)

    def conv(j):
        _conv31_rows(pbuf, dww_ref, cm_ref, pl.multiple_of(j * CONV_ROWS, CONV_ROWS))

    hm_ref[0] = up(0)

    def body(jj, carry):
        j = 2 * jj
        hm_ref[1] = up(j + 1)
        down(j, 0)
        conv(j)
        hm_ref[0] = up(j + 2)
        down(j + 1, 1)
        conv(j + 1)
        return carry

    lax.fori_loop(0, nslice // 2 - 1, body, 0)
    hm_ref[1] = up(nslice - 1)
    down(nslice - 2, 0)
    conv(nslice - 2)
    down(nslice - 1, 1)
    conv(nslice - 1)
    acc = acc_ref[...]
    if final:
        msf = jnp.mean(acc * acc, axis=-1, keepdims=True)
        acc = acc * lax.rsqrt(msf + EPS) * fnw_ref[...]
    if no == 1:
        o_refs[0][...] = acc
    else:
        b = jnp.maximum(i - 1, 0)
        first = 0
        for o_ref, end in zip(o_refs, out_ends):
            @pl.when(jnp.logical_and(b >= first, b < end))
            def _(o_ref=o_ref):
                o_ref[...] = acc
            first = end


def _mix(xs, yf, yb, z, h, gnw, dww, dwb, lnw, lnb, wout, n2w, w1, w2, fnw,
         tm, first_blocks, last_blocks, final, out_rows):
    t = sum(a.shape[0] for a in xs)
    nb = t // tm
    outs = [jax.ShapeDtypeStruct((r, D_MODEL), F32) for r in out_rows]
    hb = tm // HALO
    nh = t // HALO
    const = lambda i: (0, 0)
    done = lambda i: (jnp.maximum(i - 1, 0), 0)
    nxt_tile = lambda i: (jnp.minimum(i, nb - 1), 0)
    prev = lambda i: (jnp.maximum(jnp.minimum(i, nb - 1) * hb - 1, 0), 0)
    nxt = lambda i: (jnp.minimum((jnp.minimum(i, nb - 1) + 1) * hb, nh - 1), 0)
    once = pl.Buffered(1)
    nslice = tm // CONV_ROWS
    fs = D_FF // nslice
    w1 = w1.reshape(D_MODEL, nslice, fs).transpose(1, 0, 2).reshape(nslice * D_MODEL, fs)
    kern = functools.partial(_mix_kernel, tm=tm, nb=nb, ends=_stream_ends(xs, tm),
                             out_ends=_stream_ends(outs, tm), first_blocks=first_blocks,
                             last_blocks=last_blocks, final=final)
    return pl.pallas_call(
        kern,
        grid=(nb + 1,),
        in_specs=_stream_specs(xs, tm, lambda i: jnp.maximum(i - 1, 0)) + [
            pl.BlockSpec((tm, D_SSD), done),
            pl.BlockSpec((tm, D_SSD), done),
            pl.BlockSpec((tm, D_SSD), done),
            pl.BlockSpec((tm, D_CM), nxt_tile),
            pl.BlockSpec((HALO, D_CM), prev),
            pl.BlockSpec((HALO, D_CM), nxt),
            pl.BlockSpec((1, D_SSD), const),
            pl.BlockSpec((CM_CONV, D_CM // LANES, 2 * SUBLANES, LANES), lambda i: (0, 0, 0, 0)),
            pl.BlockSpec((SUBLANES, D_CM), const),
            pl.BlockSpec((SUBLANES, D_CM), const),
            pl.BlockSpec((SUBLANES, D_CM), const),
            pl.BlockSpec((D_SSD + D_CM, D_MODEL), const, pipeline_mode=once),
            pl.BlockSpec((1, D_MODEL), const),
            pl.BlockSpec((nslice * D_MODEL, fs), const, pipeline_mode=once),
            pl.BlockSpec((D_FF, D_MODEL), const, pipeline_mode=once),
            pl.BlockSpec((1, D_MODEL), const),
        ],
        out_specs=_stream_specs(outs, tm, lambda i: jnp.maximum(i - 1, 0)),
        out_shape=outs,
        scratch_shapes=[
            pltpu.VMEM((D_CM // LANES, tm + 2 * HALO, LANES), jnp.uint32),
            pltpu.VMEM((D_CM // LANES, tm, LANES), F32),
            pltpu.VMEM((tm, D_MODEL), BF16),
            pltpu.VMEM((tm, D_MODEL), F32),
            pltpu.VMEM((2, tm, fs), BF16),
        ],
        compiler_params=pltpu.CompilerParams(
            dimension_semantics=("arbitrary",), vmem_limit_bytes=VMEM_LIMIT),
        name="mix_mlp",
    )(*xs, yf, yb, z, h, h, h, gnw, dww, dwb, lnw, lnb, wout, n2w, w1, w2, fnw)


def _block_size(seq_lens, want):
    b = want
    while any(s % b for s in seq_lens):
        b //= 2
    assert b >= CHUNK, "sequence lengths must be multiples of the SSD chunk"
    return b


def kernel(x_prompt, x_sample, norm1_w, w_in, ssd_conv_w, ssd_conv_b, dt_bias, a_log, d_skip,
           ssd_norm_w, cm_dw_w, cm_dw_b, cm_ln_w, cm_ln_b, w_out, norm2_w, w_mlp_in, w_mlp_out,
           final_norm_w):
    depth = w_in.shape[0]
    seq_lens = [x_prompt.shape[1]] * x_prompt.shape[0] + [x_sample.shape[1]] * x_sample.shape[0]
    n_prompt = x_prompt.shape[0] * x_prompt.shape[1]
    xs = (x_prompt.reshape(-1, D_MODEL), x_sample.reshape(-1, D_MODEL))

    tb = _block_size(seq_lens, 512)
    starts = np.concatenate([[0], np.cumsum(seq_lens)[:-1]])
    ends = np.cumsum(seq_lens)
    first_blocks = tuple(int(s) // tb for s in starts)
    last_blocks = tuple(int(e) // tb - 1 for e in ends)

    s1 = D_SSD
    s2 = s1 + D_XBC
    s3 = s2 + 2 * N_HEADS
    row = lambda v: v.reshape(1, -1).astype(F32)
    for l in range(depth):
        wl = w_in[l]
        wz = wl[:, :s1].astype(BF16)
        wx = jnp.concatenate([wl[:, s1:s3], jnp.zeros((D_MODEL, DT_PAD - 2 * N_HEADS), F32)],
                             axis=1).astype(BF16)
        wcm = wl[:, s3:].astype(BF16)
        dtb = jnp.pad(dt_bias[l].reshape(1, -1), ((0, 0), (0, DT_PAD - 2 * N_HEADS)))
        alog = jnp.pad(a_log[l].reshape(1, -1), ((0, 0), (0, DT_PAD - 2 * N_HEADS)))
        z, xbc, dt, h = _inproj(xs, row(norm1_w[l]), wz, wx, wcm, dtb, tb)

        rep = lambda v: jnp.broadcast_to(v.reshape(1, -1).astype(F32), (SUBLANES, v.size))
        convw = jnp.broadcast_to(
            ssd_conv_w[l].astype(BF16).reshape(SSD_CONV, D_XBC // LANES, 1, LANES),
            (SSD_CONV, D_XBC // LANES, 2 * SUBLANES, LANES))
        drow = jnp.repeat(d_skip[l], HEAD_DIM).reshape(1, -1)
        yf, act = _ssd_fwd(xbc, dt, convw, rep(ssd_conv_b[l]), alog, drow, tb,
                           first_blocks, last_blocks)
        yb = _ssd_bwd(act, dt, alog, tb, last_blocks)

        dww = jnp.broadcast_to(cm_dw_w[l].astype(BF16).reshape(CM_CONV, D_CM // LANES, 1, LANES),
                               (CM_CONV, D_CM // LANES, 2 * SUBLANES, LANES))
        last = l == depth - 1
        n_sample = x_sample.shape[0] * x_sample.shape[1]
        out_rows = (n_prompt, n_sample) if last else (n_prompt + n_sample,)
        x = _mix(xs, yf, yb, z, h, row(ssd_norm_w[l]), dww, rep(cm_dw_b[l]), rep(cm_ln_w[l]),
                 rep(cm_ln_b[l]), w_out[l].astype(BF16), row(norm2_w[l]),
                 w_mlp_in[l].astype(BF16), w_mlp_out[l].astype(BF16), row(final_norm_w),
                 tb, first_blocks, last_blocks, last, out_rows)
        xs = tuple(x)

    y_prompt, y_sample = xs
    return y_prompt.reshape(x_prompt.shape), y_sample.reshape(x_sample.shape)
```
